```python
import math
import jax, jax.numpy as jnp
from jax import lax
import numpy as np

D_MODEL = 2048
BATCH = 16
SEQ = 256
DEPTH = 4
DEC_BATCH = 8
DEC_SEQ = 2048
PAST_LEN = 512

GRID_W = 64
N_MIXERS = 4
Q_BLOCK = 128
ROPE_THETA = 10000.0
EPS = 1e-6

N_A = (DEPTH + 3) // 4
N_B = (DEPTH + 2) // 4
N_C = (DEPTH + 1) // 4
N_D = DEPTH // 4
N_DENSE = (DEPTH + 1) // 2
N_MOE = DEPTH // 2

DIFF_HEAD_DIM = 128
DIFF_HEADS = D_MODEL // (2 * DIFF_HEAD_DIM)
DIFF_V_DIM = 2 * DIFF_HEAD_DIM
DIFF_QK_W = DIFF_HEADS * 2 * DIFF_HEAD_DIM
DIFF_V_W = DIFF_HEADS * DIFF_V_DIM

MLA_HEADS = D_MODEL // 128
MLA_Q_LORA = D_MODEL // 4
MLA_KV_LORA = D_MODEL // 8
MLA_NOPE = 128
MLA_ROPE = 64
MLA_V = 128
MLA_SCALE = (MLA_NOPE + MLA_ROPE) ** -0.5

GQA_HEAD_DIM = 128
GQA_HEADS = D_MODEL // GQA_HEAD_DIM
GQA_KV_HEADS = 4
GQA_GROUP = GQA_HEADS // GQA_KV_HEADS
GQA_Q_W = GQA_HEADS * GQA_HEAD_DIM
GQA_KV_W = GQA_KV_HEADS * GQA_HEAD_DIM

GLA_HEADS = 4
GLA_KW = D_MODEL // 2
GLA_VW = D_MODEL
GLA_DK = GLA_KW // GLA_HEADS
GLA_DV = GLA_VW // GLA_HEADS
GLA_GATE_RANK = 16
GLA_TAU = 16.0
GLA_CHUNK = 64

D_FF = 5632
N_EXPERTS = 8
TOP_K = 2
D_FF_EXPERT = 2816
MOE_BLOCK = 256

kernel_name = 'hybrid_diffusion_prefix_context_step'


def rmsnorm(x, g):
    xf = x.astype(jnp.float32)
    y = xf * lax.rsqrt(jnp.mean(xf * xf, axis=-1, keepdims=True) + EPS)
    return (y * g.astype(jnp.float32)).astype(x.dtype)


def modulation(cvec, w_mod, b_mod):
    m = jax.nn.silu(cvec) @ w_mod + b_mod
    return [t[:, None, :] for t in jnp.split(m, 6, axis=-1)]


def modulate(h, shift, scale):
    return h * (1.0 + scale) + shift


def grid_rope_tables(rows, d_rot):
    t = jnp.arange(rows * GRID_W)
    row = (t // GRID_W).astype(jnp.float32)
    col = (t % GRID_W).astype(jnp.float32)
    d_axis = d_rot // 2
    inv = ROPE_THETA ** (-jnp.arange(0, d_axis, 2, dtype=jnp.float32) / d_axis)
    ang = jnp.concatenate([row[:, None] * inv, col[:, None] * inv], axis=-1)
    return jnp.cos(ang), jnp.sin(ang)


def apply_rope(x, cos, sin):
    shp = (cos.shape[0],) + (1,) * (x.ndim - 3) + (cos.shape[1],)
    c = cos.reshape(shp).astype(x.dtype)
    s = sin.reshape(shp).astype(x.dtype)
    xp = x.reshape(x.shape[:-1] + (x.shape[-1] // 2, 2))
    x0, x1 = xp[..., 0], xp[..., 1]
    return jnp.stack([x0 * c - x1 * s, x0 * s + x1 * c], axis=-1).reshape(x.shape)


def over_query_blocks(fn, q):
    b, s = q.shape[:2]
    nb = s // Q_BLOCK
    qb = jnp.moveaxis(q.reshape((b, nb, Q_BLOCK) + q.shape[2:]), 1, 0)
    out = lax.map(fn, qb)
    return jnp.moveaxis(out, 0, 1).reshape((b, s) + out.shape[3:])


def softmax_attention(q, k, v, scale):
    def block(qb):
        s = jnp.einsum('bqhgd,bkhd->bhgqk', qb, k).astype(jnp.float32) * scale
        p = jax.nn.softmax(s, axis=-1).astype(v.dtype)
        return jnp.einsum('bhgqk,bkhe->bqhge', p, v)
    return over_query_blocks(block, q)


def diff_attention(q, k, v, lam, scale):
    def block(qb):
        s = jnp.einsum('bqhcd,bkhcd->bhcqk', qb, k).astype(jnp.float32) * scale
        p = jax.nn.softmax(s, axis=-1)
        w = (p[:, :, 0] - lam * p[:, :, 1]).astype(v.dtype)
        return jnp.einsum('bhqk,bkhe->bqhe', w, v)
    return over_query_blocks(block, q)


def diff_mixer(hp, hs, cache_k, cache_v, w_qkv, lam_p, subln_g, w_o, lam_init, cos, sin):
    def proj(h):
        b, t, _ = h.shape
        q, k, v = jnp.split(h @ w_qkv, [DIFF_QK_W, 2 * DIFF_QK_W], axis=-1)
        return (q.reshape(b, t, DIFF_HEADS, 2, DIFF_HEAD_DIM),
                k.reshape(b, t, DIFF_HEADS, 2, DIFF_HEAD_DIM),
                v.reshape(b, t, DIFF_HEADS, DIFF_V_DIM))

    lp = lam_p.astype(jnp.float32)
    lam = jnp.exp(jnp.sum(lp[0] * lp[1])) - jnp.exp(jnp.sum(lp[2] * lp[3])) + lam_init

    def out(o):
        b, t = o.shape[:2]
        return (rmsnorm(o, subln_g) * (1.0 - lam_init)).reshape(b, t, -1) @ w_o

    scale = DIFF_HEAD_DIM ** -0.5
    qp, kp, vp = proj(hp)
    o_p = out(diff_attention(qp, kp, vp, lam, scale))
    qs, ks, vs = proj(hs)
    qs, ks = apply_rope(qs, cos, sin), apply_rope(ks, cos, sin)
    k_all = jnp.concatenate([ks, cache_k], axis=1)
    v_all = jnp.concatenate([vs, cache_v], axis=1)
    o_s = out(diff_attention(qs, k_all, v_all, lam, scale))
    return o_p, o_s, kp, vp


def mla_mixer(hp, hs, cache_ckv, cache_kpe, w_down, q_norm_g, kv_norm_g, w_uq, w_ukv, w_o, cos, sin):
    def compress(h):
        b, t, _ = h.shape
        cq, ckv, kpe = jnp.split(h @ w_down, [MLA_Q_LORA, MLA_Q_LORA + MLA_KV_LORA], axis=-1)
        q = (rmsnorm(cq, q_norm_g) @ w_uq).reshape(b, t, MLA_HEADS, MLA_NOPE + MLA_ROPE)
        return q, rmsnorm(ckv, kv_norm_g), kpe

    def expand(ckv, kpe):
        b, t, _ = ckv.shape
        kv = (ckv @ w_ukv).reshape(b, t, MLA_HEADS, MLA_NOPE + MLA_V)
        k_nope, v = jnp.split(kv, [MLA_NOPE], axis=-1)
        k_pe = jnp.broadcast_to(kpe[:, :, None, :], (b, t, MLA_HEADS, MLA_ROPE))
        return jnp.concatenate([k_nope, k_pe], axis=-1), v

    def attend(q, k, v):
        b, t = q.shape[:2]
        o = softmax_attention(q[:, :, :, None, :], k, v, MLA_SCALE)
        return o.reshape(b, t, -1) @ w_o

    qp, ckv_p, kpe_p = compress(hp)
    kp, vp = expand(ckv_p, kpe_p)
    o_p = attend(qp, kp, vp)
    qs, ckv_s, kpe_s = compress(hs)
    qs = jnp.concatenate([qs[..., :MLA_NOPE], apply_rope(qs[..., MLA_NOPE:], cos, sin)], axis=-1)
    ks, vs = expand(jnp.concatenate([ckv_s, cache_ckv], axis=1),
                    jnp.concatenate([apply_rope(kpe_s, cos, sin), cache_kpe], axis=1))
    o_s = attend(qs, ks, vs)
    return o_p, o_s, ckv_p, kpe_p


def gqa_mixer(hp, hs, cache_k, cache_v, w_qkv, q_norm_g, k_norm_g, w_o, cos, sin):
    def proj(h):
        b, t, _ = h.shape
        q, k, v = jnp.split(h @ w_qkv, [GQA_Q_W, GQA_Q_W + GQA_KV_W], axis=-1)
        q = rmsnorm(q.reshape(b, t, GQA_HEADS, GQA_HEAD_DIM), q_norm_g)
        k = rmsnorm(k.reshape(b, t, GQA_KV_HEADS, GQA_HEAD_DIM), k_norm_g)
        return q, k, v.reshape(b, t, GQA_KV_HEADS, GQA_HEAD_DIM)

    def attend(q, k, v):
        b, t = q.shape[:2]
        qg = q.reshape(b, t, GQA_KV_HEADS, GQA_GROUP, GQA_HEAD_DIM)
        o = softmax_attention(qg, k, v, GQA_HEAD_DIM ** -0.5)
        return o.reshape(b, t, -1) @ w_o

    qp, kp, vp = proj(hp)
    o_p = attend(qp, kp, vp)
    qs, ks, vs = proj(hs)
    qs, ks = apply_rope(qs, cos, sin), apply_rope(ks, cos, sin)
    o_s = attend(qs, jnp.concatenate([ks, cache_k], axis=1), jnp.concatenate([vs, cache_v], axis=1))
    return o_p, o_s, kp, vp


def gla_chunked(q, k, v, g, s0):
    b, h, t, _ = q.shape
    dv = v.shape[-1]
    n = t // GLA_CHUNK

    def chunks(a):
        return jnp.moveaxis(a.reshape(b, h, n, GLA_CHUNK, a.shape[-1]), 2, 0).astype(jnp.float32)

    causal = jnp.tril(jnp.ones((GLA_CHUNK, GLA_CHUNK), dtype=bool))[:, :, None]

    def step(s, inp):
        qc, kc, vc, gc = inp
        bc = jnp.cumsum(gc, axis=2)
        decay = jnp.exp(jnp.where(causal, bc[:, :, :, None, :] - bc[:, :, None, :, :], -jnp.inf))
        att = jnp.einsum('bhid,bhjd,bhijd->bhij', qc, kc, decay)
        o = (jnp.einsum('bhcd,bhde->bhce', qc * jnp.exp(bc), s)
             + jnp.einsum('bhij,bhje->bhie', att, vc))
        b_last = bc[:, :, -1:, :]
        s = (jnp.exp(b_last[:, :, 0, :, None]) * s
             + jnp.einsum('bhcd,bhce->bhde', kc * jnp.exp(b_last - bc), vc))
        return s, o

    s, o = lax.scan(step, s0, (chunks(q), chunks(k), chunks(v), chunks(g)))
    return jnp.moveaxis(o, 0, 2).reshape(b, h, t, dv), s


def gla_mixer(hp, hs, state, w_in, w_gate2, b_gate, norm_g, w_o):
    def proj(h):
        b, t, _ = h.shape
        q, k, v, r, gl = jnp.split(
            h @ w_in, [GLA_KW, 2 * GLA_KW, 2 * GLA_KW + GLA_VW, 2 * GLA_KW + 2 * GLA_VW], axis=-1)
        heads = lambda a, d: jnp.swapaxes(a.reshape(b, t, GLA_HEADS, d), 1, 2)
        gl = gl.reshape(b, t, 2, GLA_GATE_RANK)
        glog = jax.nn.log_sigmoid(
            (jnp.einsum('btzr,zrk->btzk', gl, w_gate2) + b_gate).astype(jnp.float32)) / GLA_TAU
        return (heads(q, GLA_DK) * (GLA_DK ** -0.5), heads(k, GLA_DK), heads(v, GLA_DV), r,
                heads(glog[:, :, 0], GLA_DK), heads(glog[:, :, 1], GLA_DK))

    def bidir(q, k, v, g_f, g_b, s_f, s_b):
        flip = lambda a: jnp.flip(a, axis=2)
        o_f, s_f = gla_chunked(q, k, v, g_f, s_f)
        o_b, s_b = gla_chunked(flip(q), flip(k), flip(v), flip(g_b), s_b)
        return (o_f + flip(o_b)).astype(v.dtype), s_f, s_b

    def out(o, r):
        b, _, t, _ = o.shape
        o = rmsnorm(jnp.swapaxes(o, 1, 2), norm_g).reshape(b, t, -1)
        return (o * jax.nn.silu(r)) @ w_o

    qp, kp, vp, rp, gfp, gbp = proj(hp)
    zero = jnp.zeros((hp.shape[0], GLA_HEADS, GLA_DK, GLA_DV), jnp.float32)
    o, s_f, s_b = bidir(qp, kp, vp, gfp, gbp, zero, zero)
    o_p = out(o, rp)
    qs, ks, vs, rs, gfs, gbs = proj(hs)
    o, _, _ = bidir(qs, ks, vs, gfs, gbs, state[:, 0].astype(jnp.float32), state[:, 1].astype(jnp.float32))
    o_s = out(o, rs)
    return o_p, o_s, jnp.stack([s_f, s_b], axis=1).astype(hp.dtype)


def swiglu(h, w_gate, w_up, w_down):
    return (jax.nn.silu(h @ w_gate) * (h @ w_up)) @ w_down


def moe_swiglu(h, w_router, w_gate, w_up, w_down):
    shp = h.shape
    x = h.reshape(-1, shp[-1])
    n, d = x.shape
    logits = (x @ w_router).astype(jnp.float32)
    top_logit, top_idx = lax.top_k(logits, TOP_K)
    gates = jax.nn.softmax(top_logit, axis=-1)
    n_slots = n * TOP_K
    slot_e = top_idx.reshape(-1)
    slot_tok = jnp.arange(n_slots) // TOP_K
    order = jnp.argsort(slot_e)
    e_sorted = slot_e[order]
    tok_sorted = slot_tok[order]
    counts = jnp.zeros((N_EXPERTS,), jnp.int32).at[slot_e].add(1)
    padded = (counts + MOE_BLOCK - 1) // MOE_BLOCK * MOE_BLOCK
    pad_end = jnp.cumsum(padded)
    pad_start = pad_end - padded
    cnt_start = jnp.cumsum(counts) - counts
    dest = pad_start[e_sorted] + jnp.arange(n_slots) - cnt_start[e_sorted]
    cap = ((n_slots + MOE_BLOCK - 1) // MOE_BLOCK + N_EXPERTS) * MOE_BLOCK
    n_blocks = cap // MOE_BLOCK
    buf_tok = jnp.zeros((cap,), jnp.int32).at[dest].set(tok_sorted)
    block_start = jnp.arange(n_blocks) * MOE_BLOCK
    block_e = jnp.minimum(jnp.sum(pad_end[None, :] <= block_start[:, None], axis=-1), N_EXPERTS - 1)
    xb = x[buf_tok].reshape(n_blocks, MOE_BLOCK, d)

    def expert_block(args):
        xe, e = args
        return (jax.nn.silu(xe @ w_gate[e]) * (xe @ w_up[e])) @ w_down[e]

    yb = lax.map(expert_block, (xb, block_e)).reshape(cap, d)
    y_slots = yb[dest] * gates.reshape(-1)[order][:, None].astype(x.dtype)
    y = jnp.zeros_like(x).at[tok_sorted].add(y_slots)
    return y.reshape(shp)


def setup_inputs(seed: int = 0) -> dict:
    key = jax.random.key(seed)
    ks = iter(jax.random.split(key, 64))
    D = D_MODEL

    def nrm(shape, scale=1.0):
        return scale * jax.random.normal(next(ks), shape, jnp.float32)

    def gain(shape):
        return 1.0 + nrm(shape, 0.05)

    return {
        'x_prompt': nrm((BATCH, SEQ, D)),
        'x_sample': nrm((DEC_BATCH, DEC_SEQ, D)),
        'cache_diff_k': nrm((DEC_BATCH, N_A, PAST_LEN, DIFF_HEADS, 2, DIFF_HEAD_DIM)),
        'cache_diff_v': nrm((DEC_BATCH, N_A, PAST_LEN, DIFF_HEADS, DIFF_V_DIM)),
        'cache_mla_ckv': nrm((DEC_BATCH, N_B, PAST_LEN, MLA_KV_LORA)),
        'cache_mla_kpe': nrm((DEC_BATCH, N_B, PAST_LEN, MLA_ROPE)),
        'cache_gqa_k': nrm((DEC_BATCH, N_C, PAST_LEN, GQA_KV_HEADS, GQA_HEAD_DIM)),
        'cache_gqa_v': nrm((DEC_BATCH, N_C, PAST_LEN, GQA_KV_HEADS, GQA_HEAD_DIM)),
        'state_gla': nrm((DEC_BATCH, N_D, 2, GLA_HEADS, GLA_DK, GLA_DV)),
        'c': nrm((DEC_BATCH, D)),
        'c_ctx': nrm((D,)),
        'norm_mix_g': gain((DEPTH, D)),
        'norm_ffn_g': gain((DEPTH, D)),
        'w_mod': nrm((DEPTH, D, 6 * D), 0.5 * D ** -0.5),
        'b_mod': nrm((DEPTH, 6 * D), 0.02),
        'diff_w_qkv': nrm((N_A, D, 2 * DIFF_QK_W + DIFF_V_W), D ** -0.5),
        'diff_lambda': nrm((N_A, 4, DIFF_HEAD_DIM), 0.1),
        'diff_subln_g': gain((N_A, DIFF_V_DIM)),
        'diff_w_o': nrm((N_A, DIFF_V_W, D), DIFF_V_W ** -0.5),
        'mla_w_down': nrm((N_B, D, MLA_Q_LORA + MLA_KV_LORA + MLA_ROPE), D ** -0.5),
        'mla_q_norm_g': gain((N_B, MLA_Q_LORA)),
        'mla_kv_norm_g': gain((N_B, MLA_KV_LORA)),
        'mla_w_uq': nrm((N_B, MLA_Q_LORA, MLA_HEADS * (MLA_NOPE + MLA_ROPE)), MLA_Q_LORA ** -0.5),
        'mla_w_ukv': nrm((N_B, MLA_KV_LORA, MLA_HEADS * (MLA_NOPE + MLA_V)), MLA_KV_LORA ** -0.5),
        'mla_w_o': nrm((N_B, MLA_HEADS * MLA_V, D), (MLA_HEADS * MLA_V) ** -0.5),
        'gqa_w_qkv': nrm((N_C, D, GQA_Q_W + 2 * GQA_KV_W), D ** -0.5),
        'gqa_q_norm_g': gain((N_C, GQA_HEAD_DIM)),
        'gqa_k_norm_g': gain((N_C, GQA_HEAD_DIM)),
        'gqa_w_o': nrm((N_C, GQA_Q_W, D), GQA_Q_W ** -0.5),
        'gla_w_in': nrm((N_D, D, 2 * GLA_KW + 2 * GLA_VW + 2 * GLA_GATE_RANK), D ** -0.5),
        'gla_w_gate2': nrm((N_D, 2, GLA_GATE_RANK, GLA_KW), GLA_GATE_RANK ** -0.5),
        'gla_b_gate': nrm((N_D, 2, GLA_KW), 0.1),
        'gla_norm_g': gain((N_D, GLA_DV)),
        'gla_w_o': nrm((N_D, GLA_VW, D), GLA_VW ** -0.5),
        'ffn_w_gate': nrm((N_DENSE, D, D_FF), D ** -0.5),
        'ffn_w_up': nrm((N_DENSE, D, D_FF), D ** -0.5),
        'ffn_w_down': nrm((N_DENSE, D_FF, D), D_FF ** -0.5),
        'moe_w_router': nrm((N_MOE, D, N_EXPERTS), D ** -0.5),
        'moe_w_gate': nrm((N_MOE, N_EXPERTS, D, D_FF_EXPERT), D ** -0.5),
        'moe_w_up': nrm((N_MOE, N_EXPERTS, D, D_FF_EXPERT), D ** -0.5),
        'moe_w_down': nrm((N_MOE, N_EXPERTS, D_FF_EXPERT, D), D_FF_EXPERT ** -0.5),
        'final_norm_g': gain((D,)),
    }


def reference(x_prompt, x_sample, cache_diff_k, cache_diff_v, cache_mla_ckv, cache_mla_kpe,
              cache_gqa_k, cache_gqa_v, state_gla, c, c_ctx, norm_mix_g, norm_ffn_g, w_mod, b_mod,
              diff_w_qkv, diff_lambda, diff_subln_g, diff_w_o,
              mla_w_down, mla_q_norm_g, mla_kv_norm_g, mla_w_uq, mla_w_ukv, mla_w_o,
              gqa_w_qkv, gqa_q_norm_g, gqa_k_norm_g, gqa_w_o,
              gla_w_in, gla_w_gate2, gla_b_gate, gla_norm_g, gla_w_o,
              ffn_w_gate, ffn_w_up, ffn_w_down,
              moe_w_router, moe_w_gate, moe_w_up, moe_w_down, final_norm_g):
    rows = x_sample.shape[1] // GRID_W
    cos_h, sin_h = grid_rope_tables(rows, DIFF_HEAD_DIM)
    cos_m, sin_m = grid_rope_tables(rows, MLA_ROPE)
    xp, xs = x_prompt, x_sample
    diff_k_l, diff_v_l, mla_ckv_l, mla_kpe_l, gqa_k_l, gqa_v_l, gla_s_l = [], [], [], [], [], [], []
    for i in range(DEPTH):
        j = i // N_MIXERS
        kind = i % N_MIXERS
        mp = modulation(c_ctx[None, :], w_mod[i], b_mod[i])
        ms = modulation(c, w_mod[i], b_mod[i])
        hp = modulate(rmsnorm(xp, norm_mix_g[i]), mp[0], mp[1])
        hs = modulate(rmsnorm(xs, norm_mix_g[i]), ms[0], ms[1])
        if kind == 0:
            lam_init = 0.8 - 0.6 * math.exp(-0.3 * i)
            o_p, o_s, k_new, v_new = diff_mixer(hp, hs, cache_diff_k[:, j], cache_diff_v[:, j],
                                                diff_w_qkv[j], diff_lambda[j], diff_subln_g[j],
                                                diff_w_o[j], lam_init, cos_h, sin_h)
            diff_k_l.append(k_new)
            diff_v_l.append(v_new)
        elif kind == 1:
            o_p, o_s, ckv_new, kpe_new = mla_mixer(hp, hs, cache_mla_ckv[:, j], cache_mla_kpe[:, j],
                                                   mla_w_down[j], mla_q_norm_g[j], mla_kv_norm_g[j],
                                                   mla_w_uq[j], mla_w_ukv[j], mla_w_o[j], cos_m, sin_m)
            mla_ckv_l.append(ckv_new)
            mla_kpe_l.append(kpe_new)
        elif kind == 2:
            o_p, o_s, k_new, v_new = gqa_mixer(hp, hs, cache_gqa_k[:, j], cache_gqa_v[:, j],
                                               gqa_w_qkv[j], gqa_q_norm_g[j], gqa_k_norm_g[j],
                                               gqa_w_o[j], cos_h, sin_h)
            gqa_k_l.append(k_new)
            gqa_v_l.append(v_new)
        else:
            o_p, o_s, s_new = gla_mixer(hp, hs, state_gla[:, j], gla_w_in[j], gla_w_gate2[j],
                                        gla_b_gate[j], gla_norm_g[j], gla_w_o[j])
            gla_s_l.append(s_new)
        xp = xp + mp[2] * o_p
        xs = xs + ms[2] * o_s
        hp = modulate(rmsnorm(xp, norm_ffn_g[i]), mp[3], mp[4])
        hs = modulate(rmsnorm(xs, norm_ffn_g[i]), ms[3], ms[4])
        f = i // 2
        if i % 2 == 0:
            f_p = swiglu(hp, ffn_w_gate[f], ffn_w_up[f], ffn_w_down[f])
            f_s = swiglu(hs, ffn_w_gate[f], ffn_w_up[f], ffn_w_down[f])
        else:
            f_p = moe_swiglu(hp, moe_w_router[f], moe_w_gate[f], moe_w_up[f], moe_w_down[f])
            f_s = moe_swiglu(hs, moe_w_router[f], moe_w_gate[f], moe_w_up[f], moe_w_down[f])
        xp = xp + mp[5] * f_p
        xs = xs + ms[5] * f_s
    y_prompt = rmsnorm(xp, final_norm_g)
    y_sample = rmsnorm(xs, final_norm_g)
    new_diff_k = jnp.stack(diff_k_l, axis=1)
    new_diff_v = jnp.stack(diff_v_l, axis=1)
    new_mla_ckv = jnp.stack(mla_ckv_l, axis=1)
    new_mla_kpe = jnp.stack(mla_kpe_l, axis=1)
    new_gqa_k = jnp.stack(gqa_k_l, axis=1)
    new_gqa_v = jnp.stack(gqa_v_l, axis=1)
    new_gla_state = jnp.stack(gla_s_l, axis=1)
    return (y_prompt, y_sample, new_diff_k, new_diff_v, new_mla_ckv, new_mla_kpe, new_gqa_k, new_gqa_v, new_gla_state)
```

```python
import functools
import math

import jax
import jax.numpy as jnp
from jax import lax
from jax.experimental import pallas as pl
from jax.experimental.pallas import tpu as pltpu

F32 = jnp.float32
BF16 = jnp.bfloat16

EPS = 1e-6
GRID_W = 64
ROPE_THETA = 10000.0
N_EXPERTS = 8
GLA_GATE_RANK = 16
GLA_TAU = 16.0
GLA_CHUNK = 64

VMEM_LIMIT_BYTES = 52 * 1024 * 1024
LANES = 128

ROW_TILE = 1024
COL_TILE = 512
FFN_ROW_TILE = 512
FFN_COL_TILE = 512
MOE_ROW_TILE = 512
MOE_COL_TILE = 256
ATTN_Q_TILE = 512
DIFF_Q_TILE = 256


def _params(*sem):
    return pltpu.CompilerParams(dimension_semantics=sem, vmem_limit_bytes=VMEM_LIMIT_BYTES)


def _silu(x):
    return x / (1.0 + jnp.exp(-x))


def _rms(x, eps=EPS):
    return x * lax.rsqrt(jnp.mean(x * x, axis=-1, keepdims=True) + eps)


def _dot(a, b):
    return jnp.dot(a, b, preferred_element_type=F32)


def _dot_nt(a, b):
    return lax.dot_general(a, b, (((1,), (1,)), ((), ())), preferred_element_type=F32)


def _dot_tn(a, b):
    return lax.dot_general(a, b, (((0,), (0,)), ((), ())), preferred_element_type=F32)


def _mod_kernel(c_ref, w_ref, b_ref, o_ref):
    a = _silu(c_ref[...]).astype(BF16)
    o_ref[...] = _dot(a, w_ref[...].astype(BF16)) + b_ref[...]


def modulation_all(cvec, w_mod, b_mod):
    n_l, d, m = w_mod.shape
    g = cvec.shape[0]
    tn = 1024
    return pl.pallas_call(
        _mod_kernel,
        grid=(n_l, m // tn),
        in_specs=[
            pl.BlockSpec((g, d), lambda l, j: (0, 0)),
            pl.BlockSpec((None, d, tn), lambda l, j: (l, 0, j)),
            pl.BlockSpec((None, 1, tn), lambda l, j: (l, 0, j)),
        ],
        out_specs=pl.BlockSpec((None, g, tn), lambda l, j: (l, 0, j)),
        out_shape=jax.ShapeDtypeStruct((n_l, g, m), F32),
        compiler_params=_params("arbitrary", "arbitrary"),
        name="modulation",
    )(cvec, w_mod, b_mod.reshape(n_l, 1, m))


class Stream:
    def __init__(self, n_prompt, dec_batch, dec_seq):
        self.n_prompt = n_prompt
        self.dec_batch = dec_batch
        self.dec_seq = dec_seq
        self.n_sample = dec_batch * dec_seq
        self.n = n_prompt + self.n_sample

    def group_of_row(self, r0):
        return jnp.where(r0 < self.n_prompt, 0, 1 + (r0 - self.n_prompt) // self.dec_seq)


def _mod_spec(stream, layer, kind, d, tm, row_tile0, tn=None):
    if tn is None:
        return pl.BlockSpec(
            (None, None, 1, d),
            lambda i, j: (layer, stream.group_of_row((i + row_tile0) * tm), 0, kind))
    per = d // tn
    return pl.BlockSpec(
        (None, None, 1, tn),
        lambda i, j: (layer, stream.group_of_row((i + row_tile0) * tm), 0, kind * per + j))


def _linear_kernel(*refs, n_pro, n_epi, pro, epi, use_scratch):
    x_ref, w_ref = refs[0], refs[1]
    pro_refs = refs[2:2 + n_pro]
    epi_refs = refs[2 + n_pro:2 + n_pro + n_epi]
    o_ref = refs[2 + n_pro + n_epi]
    j = pl.program_id(1)
    if use_scratch:
        h_ref = refs[3 + n_pro + n_epi]

        @pl.when(j == 0)
        def _():
            h_ref[...] = pro(x_ref, *pro_refs)

        lhs = h_ref[...]
    else:
        lhs = x_ref[...]
    acc = _dot(lhs, w_ref[...])
    epi(acc, o_ref, j, *epi_refs)


def _epi_store(acc, o_ref, j):
    o_ref[...] = acc.astype(o_ref.dtype)


def linear(x, w, out_dtype, *, name, row_tile0=0, n_row_tiles=None, tm=ROW_TILE, tn=COL_TILE,
           pro=None, pro_args=(), epi=_epi_store, epi_args=(), out_rows=None,
           out_row_tile0=None, alias_epi_arg=None):
    k = x.shape[1]
    m = w.shape[1]
    tn = min(tn, m)
    if n_row_tiles is None:
        n_row_tiles = x.shape[0] // tm - row_tile0
    if out_row_tile0 is None:
        out_row_tile0 = row_tile0
    if out_rows is None:
        out_rows = x.shape[0]
    use_scratch = pro is not None
    kern = functools.partial(_linear_kernel, n_pro=len(pro_args), n_epi=len(epi_args),
                             pro=pro, epi=epi, use_scratch=use_scratch)
    in_specs = [
        pl.BlockSpec((tm, k), lambda i, j: (i + row_tile0, 0)),
        pl.BlockSpec((k, tn), lambda i, j: (0, j)),
    ] + [s for _, s in pro_args] + [s for _, s in epi_args]
    args = [x, w] + [a for a, _ in pro_args] + [a for a, _ in epi_args]
    aliases = {}
    if alias_epi_arg is not None:
        aliases = {2 + len(pro_args) + alias_epi_arg: 0}
    return pl.pallas_call(
        kern,
        grid=(n_row_tiles, m // tn),
        in_specs=in_specs,
        out_specs=pl.BlockSpec((tm, tn), lambda i, j: (i + out_row_tile0, j)),
        out_shape=jax.ShapeDtypeStruct((out_rows, m), out_dtype),
        scratch_shapes=[pltpu.VMEM((tm, k), BF16)] if use_scratch else [],
        input_output_aliases=aliases,
        compiler_params=_params("arbitrary", "arbitrary"),
        name=name,
    )(*args)


def _pro_norm_mod(x_ref, g_ref, shift_ref, scale_ref):
    y = _rms(x_ref[...]) * g_ref[...]
    return (y * (1.0 + scale_ref[...]) + shift_ref[...]).astype(BF16)


def norm_mod_args(stream, mod4, norm_g, layer, which, d, tm, row_tile0):
    g_spec = pl.BlockSpec((None, 1, d), lambda i, j: (layer, 0, 0))
    return (
        (norm_g, g_spec),
        (mod4, _mod_spec(stream, layer, 3 * which + 0, d, tm, row_tile0)),
        (mod4, _mod_spec(stream, layer, 3 * which + 1, d, tm, row_tile0)),
    )


def _epi_residual(acc, o_ref, j, res_ref, gate_ref):
    o_ref[...] = res_ref[...] + gate_ref[...] * acc


def _swap_pairs(x):
    n = x.shape[-1]
    lane = lax.broadcasted_iota(jnp.int32, x.shape, x.ndim - 1)
    nxt = pltpu.roll(x, n - 1, x.ndim - 1)
    prv = pltpu.roll(x, 1, x.ndim - 1)
    return jnp.where((lane & 1) == 0, nxt, prv)


def _rope(x, cos_ref, sin_ref):
    rep = x.shape[-1] // cos_ref.shape[-1]
    cos = cos_ref[...]
    sin = sin_ref[...]
    if rep > 1:
        cos = jnp.concatenate([cos] * rep, axis=-1)
        sin = jnp.concatenate([sin] * rep, axis=-1)
    return x * cos + _swap_pairs(x) * sin


def _epi_rope(acc, o_ref, j, cos_ref, sin_ref, *, n_rope_tiles):
    @pl.when(j < n_rope_tiles)
    def _():
        o_ref[...] = _rope(acc, cos_ref, sin_ref).astype(o_ref.dtype)

    @pl.when(j >= n_rope_tiles)
    def _():
        o_ref[...] = acc.astype(o_ref.dtype)


def _head_norm(acc, gain_ref):
    parts = []
    for k in range(acc.shape[-1] // LANES):
        parts.append(_rms(acc[:, k * LANES:(k + 1) * LANES]))
    return jnp.concatenate(parts, axis=-1) * gain_ref[...]


def _epi_headnorm(acc, o_ref, j, gain_ref, *, n_norm_tiles):
    @pl.when(j < n_norm_tiles)
    def _():
        o_ref[...] = _head_norm(acc, gain_ref).astype(o_ref.dtype)

    @pl.when(j >= n_norm_tiles)
    def _():
        o_ref[...] = acc.astype(o_ref.dtype)


def _epi_headnorm_rope(acc, o_ref, j, gain_ref, cos_ref, sin_ref, *, n_norm_tiles):
    @pl.when(j < n_norm_tiles)
    def _():
        o_ref[...] = _rope(_head_norm(acc, gain_ref), cos_ref, sin_ref).astype(o_ref.dtype)

    @pl.when(j >= n_norm_tiles)
    def _():
        o_ref[...] = acc.astype(o_ref.dtype)


def _epi_mla_down(acc, o_ref, j, qg_ref, kvg_ref, *rope_refs, q_lora, kv_lora):
    o_ref[:, :q_lora] = _rms(acc[:, :q_lora]) * qg_ref[...]
    o_ref[:, q_lora:q_lora + kv_lora] = _rms(acc[:, q_lora:q_lora + kv_lora]) * kvg_ref[...]
    kpe = acc[:, q_lora + kv_lora:]
    if rope_refs:
        kpe = _rope(kpe, *rope_refs)
    o_ref[:, q_lora + kv_lora:] = kpe


def _epi_rope_heads(acc, o_ref, j, cos_ref, sin_ref):
    parts = []
    for k in range(acc.shape[-1] // (2 * LANES)):
        parts.append(acc[:, 2 * k * LANES:(2 * k + 1) * LANES])
        parts.append(_rope(acc[:, (2 * k + 1) * LANES:(2 * k + 2) * LANES], cos_ref, sin_ref))
    o_ref[...] = jnp.concatenate(parts, axis=-1).astype(o_ref.dtype)


def rope_tables(rows, d_rot, width):
    t = jnp.arange(rows * GRID_W)
    row = (t // GRID_W).astype(F32)
    col = (t % GRID_W).astype(F32)
    d_axis = d_rot // 2
    inv = ROPE_THETA ** (-jnp.arange(0, d_axis, 2, dtype=F32) / d_axis)
    ang = jnp.concatenate([row[:, None] * inv, col[:, None] * inv], axis=-1)
    cos = jnp.repeat(jnp.cos(ang), 2, axis=-1)
    sin = jnp.repeat(jnp.sin(ang), 2, axis=-1)
    sign = jnp.where(jnp.arange(d_rot) % 2 == 0, -1.0, 1.0).astype(F32)
    sin = sin * sign
    if width > d_rot:
        cos = jnp.pad(cos, ((0, 0), (0, width - d_rot)), constant_values=1.0)
        sin = jnp.pad(sin, ((0, 0), (0, width - d_rot)))
    return cos, sin


def rope_args(cos, sin, stream, tm):
    per = stream.dec_seq // tm
    spec = pl.BlockSpec((tm, cos.shape[1]), lambda i, j: (i % per, 0))
    return ((cos, spec), (sin, spec))


def _softmax_parts(q, ks, scale):
    ss = [_dot_nt(q, k) * scale for k in ks]
    m = ss[0].max(axis=-1, keepdims=True)
    for s in ss[1:]:
        m = jnp.maximum(m, s.max(axis=-1, keepdims=True))
    es = [jnp.exp(s - m) for s in ss]
    l = es[0].sum(axis=-1, keepdims=True)
    for e in es[1:]:
        l = l + e.sum(axis=-1, keepdims=True)
    return es, l


def _attn_kernel(q_ref, *refs, n_seg, n_group, dq, dv, scale):
    ks = [refs[2 * s][...].astype(BF16) for s in range(n_seg)]
    vs = [refs[2 * s + 1][...].astype(BF16) for s in range(n_seg)]
    o_ref = refs[2 * n_seg]
    for g in range(n_group):
        q = q_ref[:, g * dq:(g + 1) * dq].astype(BF16)
        es, l = _softmax_parts(q, ks, scale)
        o = _dot(es[0].astype(BF16), vs[0])
        for e, v in zip(es[1:], vs[1:]):
            o = o + _dot(e.astype(BF16), v)
        o_ref[:, g * dv:(g + 1) * dv] = (o / l).astype(o_ref.dtype)


def _diff_attn_kernel(lam_ref, sub_ref, q_ref, *refs, n_seg, d, scale, lam_init):
    ks = [refs[2 * s][...].astype(BF16) for s in range(n_seg)]
    vs = [refs[2 * s + 1][...].astype(BF16) for s in range(n_seg)]
    o_ref = refs[2 * n_seg]
    lp = lam_ref[...]
    lam = (jnp.exp(jnp.sum(lp[0:1] * lp[1:2], axis=-1, keepdims=True))
           - jnp.exp(jnp.sum(lp[2:3] * lp[3:4], axis=-1, keepdims=True)) + lam_init)
    ps = []
    for c in range(2):
        q = q_ref[:, c * d:(c + 1) * d].astype(BF16)
        es, l = _softmax_parts(q, [k[:, c * d:(c + 1) * d] for k in ks], scale)
        inv = 1.0 / l
        ps.append([e * inv for e in es])
    o = None
    for s in range(n_seg):
        w = (ps[0][s] - lam * ps[1][s]).astype(BF16)
        t = _dot(w, vs[s])
        o = t if o is None else o + t
    o = _rms(o) * sub_ref[...] * (1.0 - lam_init)
    o_ref[...] = o.astype(o_ref.dtype)


def attention(q_arr, segs, *, n_batch, seq, n_kv_heads, n_group, dq, dv, q_col0, q_row0,
              out, out_row0, tq, kernel_fn, pre_args=(), name):
    n_q = seq // tq
    qw = n_group * dq
    ow = n_group * dv
    in_specs = [s for _, s in pre_args]
    args = [a for a, _ in pre_args]
    in_specs.append(pl.BlockSpec(
        (tq, qw), lambda b, h, i: ((q_row0 + b * seq) // tq + i, q_col0 // qw + h)))
    args.append(q_arr)
    for (k_arr, k_col0, dk, v_arr, v_col0, row0, t) in segs:
        in_specs.append(pl.BlockSpec(
            (t, dk), functools.partial(lambda b, h, i, row0, t, c0, w: (row0 // t + b, c0 // w + h),
                                       row0=row0, t=t, c0=k_col0, w=dk)))
        args.append(k_arr)
        in_specs.append(pl.BlockSpec(
            (t, dv), functools.partial(lambda b, h, i, row0, t, c0, w: (row0 // t + b, c0 // w + h),
                                       row0=row0, t=t, c0=v_col0, w=dv)))
        args.append(v_arr)
    in_specs.append(pl.BlockSpec(memory_space=pl.ANY))
    args.append(out)
    n_in = len(args)

    def kern(*refs):
        kernel_fn(*refs[:n_in - 1], refs[n_in])

    return pl.pallas_call(
        kern,
        grid=(n_batch, n_kv_heads, n_q),
        in_specs=in_specs,
        out_specs=pl.BlockSpec((tq, ow), lambda b, h, i: ((out_row0 + b * seq) // tq + i, h)),
        out_shape=jax.ShapeDtypeStruct(out.shape, out.dtype),
        input_output_aliases={n_in - 1: 0},
        compiler_params=_params("arbitrary", "arbitrary", "arbitrary"),
        name=name,
    )(*args)


def _ffn_kernel(x_ref, g_ref, shift_ref, scale_ref, gate_ref, wg_ref, wu_ref, wd_ref, o_ref,
                h_ref, acc_ref):
    f = pl.program_id(1)

    @pl.when(f == 0)
    def _():
        h_ref[...] = _pro_norm_mod(x_ref, g_ref, shift_ref, scale_ref)

    h = h_ref[...]
    a = _silu(_dot(h, wg_ref[...])) * _dot(h, wu_ref[...])
    part = _dot(a.astype(BF16), wd_ref[...])

    @pl.when(f == 0)
    def _():
        acc_ref[...] = part

    @pl.when(f > 0)
    def _():
        acc_ref[...] += part

    @pl.when(f == pl.num_programs(1) - 1)
    def _():
        o_ref[...] = x_ref[...] + gate_ref[...] * acc_ref[...]


def ffn_dense(x, stream, mod4, norm_g, layer, wg, wu, wd):
    n, d = x.shape
    f_dim = wg.shape[1]
    tm, tf = FFN_ROW_TILE, FFN_COL_TILE
    (g_a, g_s), (m_a, sh_s), (_, sc_s) = norm_mod_args(stream, mod4, norm_g, layer, 1, d, tm, 0)
    return pl.pallas_call(
        _ffn_kernel,
        grid=(n // tm, f_dim // tf),
        in_specs=[
            pl.BlockSpec((tm, d), lambda i, f: (i, 0)),
            g_s, sh_s, sc_s,
            _mod_spec(stream, layer, 5, d, tm, 0),
            pl.BlockSpec((d, tf), lambda i, f: (0, f)),
            pl.BlockSpec((d, tf), lambda i, f: (0, f)),
            pl.BlockSpec((tf, d), lambda i, f: (f, 0)),
        ],
        out_specs=pl.BlockSpec((tm, d), lambda i, f: (i, 0)),
        out_shape=jax.ShapeDtypeStruct((n, d), F32),
        scratch_shapes=[pltpu.VMEM((tm, d), BF16), pltpu.VMEM((tm, d), F32)],
        input_output_aliases={0: 0},
        compiler_params=_params("arbitrary", "arbitrary"),
        name="ffn_dense",
    )(x, g_a, m_a, m_a, m_a, wg, wu, wd)


def _router_kernel(x_ref, g_ref, shift_ref, scale_ref, wr_ref, h_ref, idx_ref, gate_ref):
    y = _rms(x_ref[...]) * g_ref[...]
    h = y * (1.0 + scale_ref[...]) + shift_ref[...]
    h_ref[...] = h.astype(BF16)
    logits = lax.dot_general(wr_ref[...], h, (((1,), (1,)), ((), ())),
                             precision=lax.Precision.HIGHEST,
                             preferred_element_type=F32)
    e_iota = lax.broadcasted_iota(jnp.int32, logits.shape, 0).astype(F32)
    m1 = logits.max(axis=0, keepdims=True)
    i1 = jnp.min(jnp.where(logits == m1, e_iota, float(N_EXPERTS)), axis=0, keepdims=True)
    rest = jnp.where(e_iota == i1, -jnp.inf, logits)
    m2 = rest.max(axis=0, keepdims=True)
    i2 = jnp.min(jnp.where(rest == m2, e_iota, float(N_EXPERTS)), axis=0, keepdims=True)
    e2 = jnp.exp(m2 - m1)
    den = 1.0 + e2
    idx_ref[...] = jnp.concatenate([i1, i2], axis=0).astype(jnp.int32)
    gate_ref[...] = jnp.concatenate([1.0 / den, e2 / den], axis=0)


def moe_router(x, stream, mod4, norm_g, layer, w_router_t):
    n, d = x.shape
    tm = ROW_TILE
    (g_a, g_s), (m_a, sh_s), (_, sc_s) = norm_mod_args(stream, mod4, norm_g, layer, 1, d, tm, 0)
    fix = lambda spec: pl.BlockSpec(spec.block_shape, lambda i, s=spec: s.index_map(i, 0))
    return pl.pallas_call(
        _router_kernel,
        grid=(n // tm,),
        in_specs=[
            pl.BlockSpec((tm, d), lambda i: (i, 0)),
            fix(g_s), fix(sh_s), fix(sc_s),
            pl.BlockSpec((N_EXPERTS, d), lambda i: (0, 0)),
        ],
        out_specs=[
            pl.BlockSpec((tm, d), lambda i: (i, 0)),
            pl.BlockSpec((2, tm), lambda i: (0, i)),
            pl.BlockSpec((2, tm), lambda i: (0, i)),
        ],
        out_shape=[
            jax.ShapeDtypeStruct((n, d), BF16),
            jax.ShapeDtypeStruct((2, n), jnp.int32),
            jax.ShapeDtypeStruct((2, n), F32),
        ],
        compiler_params=_params("arbitrary"),
        name="moe_router",
    )(x, g_a, m_a, m_a, w_router_t)


def _moe_ffn_kernel(be_ref, nb_ref, x_ref, wg_ref, wu_ref, wd_ref, o_ref, acc_ref):
    i = pl.program_id(0)
    f = pl.program_id(1)

    @pl.when(i < nb_ref[0])
    def _():
        h = x_ref[...]
        a = _silu(_dot(h, wg_ref[...])) * _dot(h, wu_ref[...])
        part = _dot(a.astype(BF16), wd_ref[...])

        @pl.when(f == 0)
        def _():
            acc_ref[...] = part

        @pl.when(f > 0)
        def _():
            acc_ref[...] += part

    last = f == pl.num_programs(1) - 1

    @pl.when(last & (i < nb_ref[0]))
    def _():
        o_ref[...] = acc_ref[...].astype(o_ref.dtype)

    @pl.when(last & (i >= nb_ref[0]))
    def _():
        o_ref[...] = jnp.zeros(o_ref.shape, o_ref.dtype)


def moe_experts(xb, block_e, n_used, wg, wu, wd):
    cap, d = xb.shape
    f_dim = wg.shape[2]
    tm, tf = MOE_ROW_TILE, MOE_COL_TILE
    n_f = f_dim // tf

    def wcol(i, f, be, nb):
        live = i < nb[0]
        return (be[i], 0, jnp.where(live, f, n_f - 1))

    def wrow(i, f, be, nb):
        live = i < nb[0]
        return (be[i], jnp.where(live, f, n_f - 1), 0)

    grid_spec = pltpu.PrefetchScalarGridSpec(
        num_scalar_prefetch=2,
        grid=(cap // tm, n_f),
        in_specs=[
            pl.BlockSpec((tm, d), lambda i, f, be, nb: (jnp.minimum(i, nb[0] - 1), 0)),
            pl.BlockSpec((None, d, tf), wcol),
            pl.BlockSpec((None, d, tf), wcol),
            pl.BlockSpec((None, tf, d), wrow),
        ],
        out_specs=pl.BlockSpec((tm, d), lambda i, f, be, nb: (i, 0)),
        scratch_shapes=[pltpu.VMEM((tm, d), F32)],
    )
    return pl.pallas_call(
        _moe_ffn_kernel,
        grid_spec=grid_spec,
        out_shape=jax.ShapeDtypeStruct((cap, d), F32),
        compiler_params=_params("arbitrary", "arbitrary"),
        name="moe_experts",
    )(block_e, n_used, xb, wg, wu, wd)


def _moe_combine_kernel(x_ref, gate_ref, y0_ref, y1_ref, w_ref, o_ref):
    w = w_ref[...]
    y = y0_ref[...] * w[:, 0:1] + y1_ref[...] * w[:, 1:2]
    o_ref[...] = x_ref[...] + gate_ref[...] * y


def moe_combine(x, stream, mod4, layer, y0, y1, gates_t):
    n, d = x.shape
    tm = FFN_ROW_TILE
    gspec = _mod_spec(stream, layer, 5, d, tm, 0)
    gspec = pl.BlockSpec(gspec.block_shape, lambda i, s=gspec: s.index_map(i, 0))
    row = pl.BlockSpec((tm, d), lambda i: (i, 0))
    return pl.pallas_call(
        _moe_combine_kernel,
        grid=(n // tm,),
        in_specs=[row, gspec, row, row, pl.BlockSpec((tm, 2), lambda i: (i, 0))],
        out_specs=row,
        out_shape=jax.ShapeDtypeStruct((n, d), F32),
        input_output_aliases={0: 0},
        compiler_params=_params("arbitrary"),
        name="moe_combine",
    )(x, mod4, y0, y1, gates_t)


def moe_layer(x, stream, mod4, norm_g, layer, w_router, wg, wu, wd):
    n, d = x.shape
    tm = MOE_ROW_TILE
    h, idx, gates = moe_router(x, stream, mod4, norm_g, layer, w_router.T)
    slot_e = idx.reshape(-1)
    n_slots = slot_e.shape[0]
    onehot = (slot_e[:, None] == jnp.arange(N_EXPERTS)[None, :]).astype(jnp.int32)
    rank = jnp.take_along_axis(jnp.cumsum(onehot, axis=0) - onehot, slot_e[:, None], axis=1)[:, 0]
    counts = onehot.sum(axis=0)
    padded = (counts + tm - 1) // tm * tm
    pad_end = jnp.cumsum(padded)
    pad_start = pad_end - padded
    dest = pad_start[slot_e] + rank
    cap = (n_slots // tm + N_EXPERTS) * tm
    n_blocks = cap // tm
    slot_tok = jnp.arange(n_slots, dtype=jnp.int32) % n
    buf_tok = jnp.zeros((cap,), jnp.int32).at[dest].set(slot_tok)
    block_start = jnp.arange(n_blocks) * tm
    block_e = jnp.minimum(jnp.sum(pad_end[None, :] <= block_start[:, None], axis=-1),
                          N_EXPERTS - 1).astype(jnp.int32)
    n_used = (pad_end[-1] // tm).astype(jnp.int32).reshape(1)
    xb = jnp.take(h, buf_tok, axis=0)
    yb = moe_experts(xb, block_e, n_used, wg, wu, wd)
    y0 = jnp.take(yb, dest[:n], axis=0)
    y1 = jnp.take(yb, dest[n:], axis=0)
    return moe_combine(x, stream, mod4, layer, y0, y1, gates.T)


def _gla_gate_kernel(gl_ref, w_ref, b_ref, gf_ref, gb_ref):
    gl = gl_ref[...]
    for z, o_ref in enumerate((gf_ref, gb_ref)):
        a = gl[:, z * GLA_GATE_RANK:(z + 1) * GLA_GATE_RANK]
        t = jnp.dot(a, w_ref[z], precision=lax.Precision.HIGHEST,
                    preferred_element_type=F32) + b_ref[z]
        o_ref[...] = (jnp.minimum(t, 0.0) - jnp.log1p(jnp.exp(-jnp.abs(t)))) / GLA_TAU


def gla_gates(gl, w_gate2, b_gate):
    n = gl.shape[0]
    kw = w_gate2.shape[2]
    tm = ROW_TILE
    out = jax.ShapeDtypeStruct((n, kw), F32)
    return pl.pallas_call(
        _gla_gate_kernel,
        grid=(n // tm,),
        in_specs=[
            pl.BlockSpec((tm, gl.shape[1]), lambda i: (i, 0)),
            pl.BlockSpec(w_gate2.shape, lambda i: (0, 0, 0)),
            pl.BlockSpec((2, 1, kw), lambda i: (0, 0, 0)),
        ],
        out_specs=[pl.BlockSpec((tm, kw), lambda i: (i, 0))] * 2,
        out_shape=[out, out],
        compiler_params=_params("arbitrary"),
        name="gla_gates",
    )(gl, w_gate2, b_gate.reshape(2, 1, kw))


def _cumsum_rows(g, tri):
    hi = g.astype(BF16)
    r1 = g - hi.astype(F32)
    mid = r1.astype(BF16)
    lo = (r1 - mid.astype(F32)).astype(BF16)
    return _dot(tri, hi) + _dot(tri, mid) + _dot(tri, lo)


def _row_to_col(x):
    n = x.shape[1]
    eye = lax.broadcasted_iota(jnp.int32, (n, n), 0) == lax.broadcasted_iota(jnp.int32, (n, n), 1)
    return jnp.sum(jnp.where(eye, jnp.broadcast_to(x, (n, n)), 0.0), axis=1, keepdims=True)


def _gla_chunk(q, k, v, g, s, *, reverse):
    c = q.shape[0]
    r_i = lax.broadcasted_iota(jnp.int32, (c, c), 0)
    c_i = lax.broadcasted_iota(jnp.int32, (c, c), 1)
    earlier = (c_i >= r_i) if reverse else (c_i <= r_i)
    tri = jnp.where(earlier, 1.0, 0.0).astype(BF16)
    b = _cumsum_rows(g, tri)
    b_tot = b[0:1] if reverse else b[c - 1:c]
    o = _dot((q * jnp.exp(b)).astype(BF16), s.astype(BF16))

    lane = lax.broadcasted_iota(jnp.int32, (c, LANES), 1)

    def col(j, att):
        sel = (lax.broadcasted_iota(jnp.int32, (c, 1), 0) == j).astype(F32)
        kj = jnp.sum(k * sel, axis=0, keepdims=True)
        bj = jnp.sum(b * sel, axis=0, keepdims=True)
        w = q * kj * jnp.exp(jnp.minimum(b - bj, 0.0))
        return jnp.where(lane == j, jnp.sum(w, axis=-1, keepdims=True), att)

    att = lax.fori_loop(0, c, col, jnp.zeros((c, LANES), F32))
    att = jnp.where(earlier, att[:, :c], 0.0)
    o = o + _dot(att.astype(BF16), v)
    k_dec = (k * jnp.exp(b_tot - b)).astype(BF16)
    s_new = _row_to_col(jnp.exp(b_tot)) * s + _dot_tn(k_dec, v)
    return o, s_new


def _gla_kernel(q_ref, k_ref, v_ref, r_ref, gf_ref, gb_ref, s0_ref, ng_ref, o_ref, s_out_ref,
                acc_ref, *, seq, scale):
    n_chunks = seq // GLA_CHUNK

    def run(reverse, g_ref, s_init):
        def body(t, s):
            ci = (n_chunks - 1 - t) if reverse else t
            rows = pl.ds(pl.multiple_of(ci * GLA_CHUNK, GLA_CHUNK), GLA_CHUNK)
            q = q_ref[rows, :].astype(F32) * scale
            k = k_ref[rows, :].astype(F32)
            v = v_ref[rows, :].astype(BF16)
            o, s = _gla_chunk(q, k, v, g_ref[rows, :], s, reverse=reverse)
            if reverse:
                acc_ref[rows, :] += o
            else:
                acc_ref[rows, :] = o
            return s

        return lax.fori_loop(0, n_chunks, body, s_init)

    s_out_ref[0] = run(False, gf_ref, s0_ref[0])
    s_out_ref[1] = run(True, gb_ref, s0_ref[1])
    o = _rms(acc_ref[...]) * ng_ref[...]
    o_ref[...] = (o * _silu(r_ref[...].astype(F32))).astype(o_ref.dtype)


def gla_core(proj, gf, gb, state, norm_g, *, n_batch, seq, n_heads, dk, dv, row0, g_row0, out,
             out_row0, name):
    kw = n_heads * dk
    vw = n_heads * dv
    rb = row0 // seq
    gr = g_row0 // seq
    ob = out_row0 // seq
    in_specs = [
        pl.BlockSpec((seq, dk), lambda b, h: (rb + b, h)),
        pl.BlockSpec((seq, dk), lambda b, h: (rb + b, kw // dk + h)),
        pl.BlockSpec((seq, dv), lambda b, h: (rb + b, 2 * kw // dv + h)),
        pl.BlockSpec((seq, dv), lambda b, h: (rb + b, (2 * kw + vw) // dv + h)),
        pl.BlockSpec((seq, dk), lambda b, h: (gr + b, h)),
        pl.BlockSpec((seq, dk), lambda b, h: (gr + b, h)),
        pl.BlockSpec((None, 2, None, dk, dv), lambda b, h: (b, 0, h, 0, 0)),
        pl.BlockSpec((1, dv), lambda b, h: (0, 0)),
        pl.BlockSpec(memory_space=pl.ANY),
    ]

    def kern(q, k, v, r, gf_, gb_, s0, ng, _alias, o, s_out, acc):
        _gla_kernel(q, k, v, r, gf_, gb_, s0, ng, o, s_out, acc, seq=seq, scale=dk ** -0.5)

    return pl.pallas_call(
        kern,
        grid=(n_batch, n_heads),
        in_specs=in_specs,
        out_specs=[
            pl.BlockSpec((seq, dv), lambda b, h: (ob + b, h)),
            pl.BlockSpec((None, 2, None, dk, dv), lambda b, h: (b, 0, h, 0, 0)),
        ],
        out_shape=[
            jax.ShapeDtypeStruct(out.shape, out.dtype),
            jax.ShapeDtypeStruct(state.shape, F32),
        ],
        scratch_shapes=[pltpu.VMEM((seq, dv), F32)],
        input_output_aliases={8: 0},
        compiler_params=_params("arbitrary", "arbitrary"),
        name=name,
    )(proj, proj, proj, proj, gf, gb, state, norm_g.reshape(1, dv), out)


def _final_norm_kernel(x_ref, g_ref, o_ref):
    o_ref[...] = _rms(x_ref[...]) * g_ref[...]


def final_norm(x, g, row0, rows):
    d = x.shape[1]
    tm = ROW_TILE
    return pl.pallas_call(
        _final_norm_kernel,
        grid=(rows // tm,),
        in_specs=[pl.BlockSpec((tm, d), lambda i: (row0 // tm + i, 0)),
                  pl.BlockSpec((1, d), lambda i: (0, 0))],
        out_specs=pl.BlockSpec((tm, d), lambda i: (i, 0)),
        out_shape=jax.ShapeDtypeStruct((rows, d), F32),
        compiler_params=_params("arbitrary"),
        name="final_norm",
    )(x, g.reshape(1, d))


def kernel(x_prompt, x_sample, cache_diff_k, cache_diff_v, cache_mla_ckv, cache_mla_kpe,
           cache_gqa_k, cache_gqa_v, state_gla, c, c_ctx, norm_mix_g, norm_ffn_g, w_mod, b_mod,
           diff_w_qkv, diff_lambda, diff_subln_g, diff_w_o,
           mla_w_down, mla_q_norm_g, mla_kv_norm_g, mla_w_uq, mla_w_ukv, mla_w_o,
           gqa_w_qkv, gqa_q_norm_g, gqa_k_norm_g, gqa_w_o,
           gla_w_in, gla_w_gate2, gla_b_gate, gla_norm_g, gla_w_o,
           ffn_w_gate, ffn_w_up, ffn_w_down,
           moe_w_router, moe_w_gate, moe_w_up, moe_w_down, final_norm_g):
    n_b, seq, d = x_prompt.shape
    dec_b, dec_seq, _ = x_sample.shape
    depth = norm_mix_g.shape[0]
    past = cache_diff_k.shape[2]
    stream = Stream(n_b * seq, dec_b, dec_seq)
    n_p, n_s, n = stream.n_prompt, stream.n_sample, stream.n
    tm = ROW_TILE
    p_tiles, s_tiles = n_p // tm, n_s // tm
    rows = dec_seq // GRID_W
    assert n_p % dec_seq == 0 and n_p % tm == 0 and dec_seq % tm == 0 and seq % GLA_CHUNK == 0

    x = jnp.concatenate([x_prompt.reshape(n_p, d), x_sample.reshape(n_s, d)], axis=0)

    n_groups = 16
    cvec = jnp.zeros((n_groups, d), F32).at[0].set(c_ctx).at[1:1 + dec_b].set(c)
    mod = modulation_all(cvec, w_mod, b_mod)
    mod4 = mod.reshape(depth, n_groups, 1, 6 * d)
    norm_mix3 = norm_mix_g.reshape(depth, 1, d)
    norm_ffn3 = norm_ffn_g.reshape(depth, 1, d)

    cos_h, sin_h = rope_tables(rows, 128, 128)
    cos_m, sin_m = rope_tables(rows, 64, 128)

    def mix_in(layer, w, out_dtype_p, out_dtype_s, epi_p, epi_args_p, epi_s, epi_args_s, name,
               tn=COL_TILE):
        outs = []
        for (t0, nt, odt, epi, eargs, tag) in (
                (0, p_tiles, out_dtype_p, epi_p, epi_args_p, "p"),
                (p_tiles, s_tiles, out_dtype_s, epi_s, epi_args_s, "s")):
            outs.append(linear(
                x, w, odt, name=f"{name}_{tag}", row_tile0=t0, n_row_tiles=nt, tn=tn,
                pro=_pro_norm_mod,
                pro_args=norm_mod_args(stream, mod4, norm_mix3, layer, 0, d, tm, t0),
                epi=epi, epi_args=eargs, out_rows=nt * tm, out_row_tile0=0))
        return outs

    def mix_out(layer, o, w_o, name):
        return linear(
            o, w_o, F32, name=name, epi=_epi_residual,
            epi_args=((x, pl.BlockSpec((tm, COL_TILE), lambda i, j: (i, j))),
                      (mod4, _mod_spec(stream, layer, 2, d, tm, 0, tn=COL_TILE))),
            alias_epi_arg=0)

    diff_k_l, diff_v_l, mla_ckv_l, mla_kpe_l, gqa_k_l, gqa_v_l, gla_s_l = [], [], [], [], [], [], []
    for i in range(depth):
        j = i // 4
        kind = i % 4
        o_buf = jnp.zeros((n, d), BF16)
        if kind == 0:
            hd = diff_lambda.shape[-1]
            n_h = d // (2 * hd)
            qk_w = n_h * 2 * hd
            lam_init = 0.8 - 0.6 * math.exp(-0.3 * i)
            w = diff_w_qkv[j].astype(BF16)
            qkv_p, qkv_s = mix_in(
                i, w, F32, BF16, _epi_store, (),
                functools.partial(_epi_rope, n_rope_tiles=2 * qk_w // COL_TILE),
                rope_args(cos_h, sin_h, stream, tm), "diff_qkv")
            diff_k_l.append(qkv_p[:, qk_w:2 * qk_w].reshape(n_b, seq, n_h, 2, hd))
            diff_v_l.append(qkv_p[:, 2 * qk_w:].reshape(n_b, seq, n_h, 2 * hd))
            ck = cache_diff_k[:, j].reshape(dec_b * past, qk_w)
            cv = cache_diff_v[:, j].reshape(dec_b * past, qk_w)
            pre = ((diff_lambda[j], pl.BlockSpec((4, hd), lambda b, h, t: (0, 0))),
                   (diff_subln_g[j].reshape(1, 2 * hd), pl.BlockSpec((1, 2 * hd), lambda b, h, t: (0, 0))))
            scale = hd ** -0.5
            o_buf = attention(
                qkv_p, [(qkv_p, qk_w, 2 * hd, qkv_p, 2 * qk_w, 0, seq)],
                n_batch=n_b, seq=seq, n_kv_heads=n_h, n_group=1, dq=2 * hd, dv=2 * hd, q_col0=0,
                q_row0=0, out=o_buf, out_row0=0, tq=seq, pre_args=pre, name="diff_attn_p",
                kernel_fn=functools.partial(_diff_attn_kernel, n_seg=1, d=hd, scale=scale,
                                            lam_init=lam_init))
            o_buf = attention(
                qkv_s, [(qkv_s, qk_w, 2 * hd, qkv_s, 2 * qk_w, 0, dec_seq),
                        (ck, 0, 2 * hd, cv, 0, 0, past)],
                n_batch=dec_b, seq=dec_seq, n_kv_heads=n_h, n_group=1, dq=2 * hd, dv=2 * hd,
                q_col0=0, q_row0=0, out=o_buf, out_row0=n_p, tq=DIFF_Q_TILE, pre_args=pre,
                name="diff_attn_s",
                kernel_fn=functools.partial(_diff_attn_kernel, n_seg=2, d=hd, scale=scale,
                                            lam_init=lam_init))
            x = mix_out(i, o_buf, diff_w_o[j].astype(BF16), "diff_out")
        elif kind == 1:
            q_lora = mla_q_norm_g.shape[1]
            kv_lora = mla_kv_norm_g.shape[1]
            d_rope = cache_mla_kpe.shape[-1]
            n_h = d // 128
            d_nope = mla_w_uq.shape[2] // n_h - d_rope
            d_v = mla_w_ukv.shape[2] // n_h - d_nope
            hw = 2 * LANES
            w_down = jnp.pad(mla_w_down[j], ((0, 0), (0, LANES - d_rope))).astype(BF16)
            dw = w_down.shape[1]
            g_args = ((mla_q_norm_g[j].reshape(1, q_lora), pl.BlockSpec((1, q_lora), lambda i_, j_: (0, 0))),
                      (mla_kv_norm_g[j].reshape(1, kv_lora), pl.BlockSpec((1, kv_lora), lambda i_, j_: (0, 0))))
            epi_down = functools.partial(_epi_mla_down, q_lora=q_lora, kv_lora=kv_lora)
            down_p, down_s = mix_in(i, w_down, F32, F32, epi_down, g_args, epi_down,
                                    g_args + rope_args(cos_m, sin_m, stream, tm), "mla_down", tn=dw)
            mla_ckv_l.append(down_p[:, q_lora:q_lora + kv_lora].reshape(n_b, seq, kv_lora))
            mla_kpe_l.append(down_p[:, q_lora + kv_lora:q_lora + kv_lora + d_rope].reshape(n_b, seq, d_rope))
            wq = mla_w_uq[j].reshape(q_lora, n_h, d_nope + d_rope)
            wq = jnp.pad(wq, ((0, 0), (0, 0), (0, hw - d_nope - d_rope))).reshape(q_lora, n_h * hw).astype(BF16)
            wkv = mla_w_ukv[j].reshape(kv_lora, n_h, d_nope + d_v)
            wk = jnp.pad(wkv[:, :, :d_nope], ((0, 0), (0, 0), (0, hw - d_nope)))
            eye = jnp.pad(jnp.eye(d_rope, dtype=F32), ((0, 0), (d_nope, hw - d_nope - d_rope)))
            wk = jnp.concatenate([wk, jnp.broadcast_to(eye[:, None, :], (d_rope, n_h, hw))], axis=0)
            wk = wk.reshape(kv_lora + d_rope, n_h * hw).astype(BF16)
            wv = wkv[:, :, d_nope:].reshape(kv_lora, n_h * d_v).astype(BF16)
            q_p = linear(down_p[:, :q_lora].astype(BF16), wq, BF16, name="mla_uq_p")
            q_s = linear(down_s[:, :q_lora].astype(BF16), wq, BF16, name="mla_uq_s",
                         epi=_epi_rope_heads, epi_args=rope_args(cos_m, sin_m, stream, tm))
            u_p = down_p[:, q_lora:q_lora + kv_lora + d_rope]
            u_new = down_s[:, q_lora:q_lora + kv_lora + d_rope].reshape(dec_b, dec_seq, -1)
            u_old = jnp.concatenate([cache_mla_ckv[:, j], cache_mla_kpe[:, j]], axis=-1)
            t_all = dec_seq + past
            u_s = jnp.concatenate([u_new, u_old], axis=1).reshape(dec_b * t_all, -1)
            ukv_tm = 512
            k_p = linear(u_p.astype(BF16), wk, BF16, name="mla_uk_p", tm=ukv_tm)
            v_p = linear(u_p[:, :kv_lora].astype(BF16), wv, BF16, name="mla_uv_p", tm=ukv_tm)
            k_s = linear(u_s.astype(BF16), wk, BF16, name="mla_uk_s", tm=ukv_tm)
            v_s = linear(u_s[:, :kv_lora].astype(BF16), wv, BF16, name="mla_uv_s", tm=ukv_tm)
            scale = (d_nope + d_rope) ** -0.5
            kf = functools.partial(_attn_kernel, n_seg=1, n_group=1, dq=hw, dv=d_v, scale=scale)
            o_buf = attention(
                q_p, [(k_p, 0, hw, v_p, 0, 0, seq)],
                n_batch=n_b, seq=seq, n_kv_heads=n_h, n_group=1, dq=hw, dv=d_v, q_col0=0, q_row0=0,
                out=o_buf, out_row0=0, tq=seq, kernel_fn=kf, name="mla_attn_p")
            o_buf = attention(
                q_s, [(k_s, 0, hw, v_s, 0, 0, t_all)],
                n_batch=dec_b, seq=dec_seq, n_kv_heads=n_h, n_group=1, dq=hw, dv=d_v, q_col0=0,
                q_row0=0, out=o_buf, out_row0=n_p, tq=ATTN_Q_TILE, kernel_fn=kf, name="mla_attn_s")
            x = mix_out(i, o_buf, mla_w_o[j].astype(BF16), "mla_out")
        elif kind == 2:
            hd = gqa_q_norm_g.shape[1]
            n_kv = cache_gqa_k.shape[3]
            n_h = d // hd
            grp = n_h // n_kv
            q_w, kv_w = n_h * hd, n_kv * hd
            w = gqa_w_qkv[j].astype(BF16)
            gain = jnp.concatenate([jnp.tile(gqa_q_norm_g[j], n_h), jnp.tile(gqa_k_norm_g[j], n_kv),
                                    jnp.ones((kv_w,), F32)]).reshape(1, -1)
            gain_arg = ((gain, pl.BlockSpec((1, COL_TILE), lambda i_, j_: (0, j_))),)
            n_norm = (q_w + kv_w) // COL_TILE
            qkv_p, qkv_s = mix_in(
                i, w, F32, BF16,
                functools.partial(_epi_headnorm, n_norm_tiles=n_norm), gain_arg,
                functools.partial(_epi_headnorm_rope, n_norm_tiles=n_norm),
                gain_arg + rope_args(cos_h, sin_h, stream, tm), "gqa_qkv")
            gqa_k_l.append(qkv_p[:, q_w:q_w + kv_w].reshape(n_b, seq, n_kv, hd))
            gqa_v_l.append(qkv_p[:, q_w + kv_w:].reshape(n_b, seq, n_kv, hd))
            ck = cache_gqa_k[:, j].reshape(dec_b * past, kv_w)
            cv = cache_gqa_v[:, j].reshape(dec_b * past, kv_w)
            scale = hd ** -0.5
            o_buf = attention(
                qkv_p, [(qkv_p, q_w, hd, qkv_p, q_w + kv_w, 0, seq)],
                n_batch=n_b, seq=seq, n_kv_heads=n_kv, n_group=grp, dq=hd, dv=hd, q_col0=0, q_row0=0,
                out=o_buf, out_row0=0, tq=seq, name="gqa_attn_p",
                kernel_fn=functools.partial(_attn_kernel, n_seg=1, n_group=grp, dq=hd, dv=hd, scale=scale))
            o_buf = attention(
                qkv_s, [(qkv_s, q_w, hd, qkv_s, q_w + kv_w, 0, dec_seq), (ck, 0, hd, cv, 0, 0, past)],
                n_batch=dec_b, seq=dec_seq, n_kv_heads=n_kv, n_group=grp, dq=hd, dv=hd, q_col0=0,
                q_row0=0, out=o_buf, out_row0=n_p, tq=ATTN_Q_TILE, name="gqa_attn_s",
                kernel_fn=functools.partial(_attn_kernel, n_seg=2, n_group=grp, dq=hd, dv=hd, scale=scale))
            x = mix_out(i, o_buf, gqa_w_o[j].astype(BF16), "gqa_out")
        else:
            n_h = state_gla.shape[3]
            dk, dv = state_gla.shape[4], state_gla.shape[5]
            kw, vw = n_h * dk, n_h * dv
            main_w = 2 * kw + 2 * vw
            w_in = gla_w_in[j]
            w_main = w_in[:, :main_w].astype(BF16)
            w_gl = jnp.pad(w_in[:, main_w:], ((0, 0), (0, LANES - 2 * GLA_GATE_RANK))).astype(BF16)
            pro_all = norm_mod_args(stream, mod4, norm_mix3, i, 0, d, tm, 0)
            proj = linear(x, w_main, BF16, name="gla_in", pro=_pro_norm_mod, pro_args=pro_all)
            gl = linear(x, w_gl, F32, name="gla_in_gate", pro=_pro_norm_mod, pro_args=pro_all)
            gf, gb = gla_gates(gl, gla_w_gate2[j], gla_b_gate[j])
            zero = jnp.zeros((n_b, 2, n_h, dk, dv), F32)
            o_buf, s_new = gla_core(proj, gf, gb, zero, gla_norm_g[j], n_batch=n_b, seq=seq,
                                    n_heads=n_h, dk=dk, dv=dv, row0=0, g_row0=0, out=o_buf,
                                    out_row0=0, name="gla_p")
            o_buf, _ = gla_core(proj, gf, gb, state_gla[:, j], gla_norm_g[j], n_batch=dec_b,
                                seq=dec_seq, n_heads=n_h, dk=dk, dv=dv, row0=n_p, g_row0=n_p,
                                out=o_buf, out_row0=n_p, name="gla_s")
            gla_s_l.append(s_new)
            x = mix_out(i, o_buf, gla_w_o[j].astype(BF16), "gla_out")

        f = i // 2
        if i % 2 == 0:
            x = ffn_dense(x, stream, mod4, norm_ffn3, i, ffn_w_gate[f].astype(BF16),
                          ffn_w_up[f].astype(BF16), ffn_w_down[f].astype(BF16))
        else:
            x = moe_layer(x, stream, mod4, norm_ffn3, i, moe_w_router[f], moe_w_gate[f].astype(BF16),
                          moe_w_up[f].astype(BF16), moe_w_down[f].astype(BF16))

    y_prompt = final_norm(x, final_norm_g, 0, n_p).reshape(n_b, seq, d)
    y_sample = final_norm(x, final_norm_g, n_p, n_s).reshape(dec_b, dec_seq, d)
    return (y_prompt, y_sample,
            jnp.stack(diff_k_l, axis=1), jnp.stack(diff_v_l, axis=1),
            jnp.stack(mla_ckv_l, axis=1), jnp.stack(mla_kpe_l, axis=1),
            jnp.stack(gqa_k_l, axis=1), jnp.stack(gqa_v_l, axis=1),
            jnp.stack(gla_s_l, axis=1))
```

```python
import functools
import math

import jax
import jax.numpy as jnp
from jax import lax
from jax.experimental import pallas as pl
from jax.experimental.pallas import tpu as pltpu

F32 = jnp.float32
BF16 = jnp.bfloat16

EPS = 1e-6
GRID_W = 64
ROPE_THETA = 10000.0
N_EXPERTS = 8
GLA_GATE_RANK = 16
GLA_TAU = 16.0
GLA_CHUNK = 64
GLA_SUB = 16

VMEM_LIMIT_BYTES = 52 * 1024 * 1024
LANES = 128

ROW_TILE = 1024
COL_TILE = 512
FFN_ROW_TILE = 512
FFN_COL_TILE = 512
MOE_ROW_TILE = 512
MOE_COL_TILE = 256
ATTN_Q_ROWS = 512
ATTN_KEY_BLOCK = 512


def _params(*sem):
    return pltpu.CompilerParams(dimension_semantics=sem, vmem_limit_bytes=VMEM_LIMIT_BYTES)


def _silu(x):
    return x / (1.0 + jnp.exp(-x))


def _rms(x, eps=EPS):
    return x * lax.rsqrt(jnp.mean(x * x, axis=-1, keepdims=True) + eps)


def _dot(a, b):
    return jnp.dot(a, b, preferred_element_type=F32)


def _dot_nt(a, b):
    return lax.dot_general(a, b, (((1,), (1,)), ((), ())), preferred_element_type=F32)


def _dot_tn(a, b):
    return lax.dot_general(a, b, (((0,), (0,)), ((), ())), preferred_element_type=F32)


def _mod_kernel(c_ref, w_ref, b_ref, o_ref):
    a = _silu(c_ref[...]).astype(BF16)
    o_ref[...] = _dot(a, w_ref[...].astype(BF16)) + b_ref[...]


def modulation_all(cvec, w_mod, b_mod):
    n_l, d, m = w_mod.shape
    g = cvec.shape[0]
    tn = 1024
    return pl.pallas_call(
        _mod_kernel,
        grid=(n_l, m // tn),
        in_specs=[
            pl.BlockSpec((g, d), lambda l, j: (0, 0)),
            pl.BlockSpec((None, d, tn), lambda l, j: (l, 0, j)),
            pl.BlockSpec((None, 1, tn), lambda l, j: (l, 0, j)),
        ],
        out_specs=pl.BlockSpec((None, g, tn), lambda l, j: (l, 0, j)),
        out_shape=jax.ShapeDtypeStruct((n_l, g, m), F32),
        compiler_params=_params("arbitrary", "arbitrary"),
        name="modulation",
    )(cvec, w_mod, b_mod.reshape(n_l, 1, m))


class Stream:
    def __init__(self, n_prompt, dec_batch, dec_seq):
        self.n_prompt = n_prompt
        self.dec_batch = dec_batch
        self.dec_seq = dec_seq
        self.n_sample = dec_batch * dec_seq
        self.n = n_prompt + self.n_sample

    def group_of_row(self, r0):
        return jnp.where(r0 < self.n_prompt, 0, 1 + (r0 - self.n_prompt) // self.dec_seq)


def _mod_spec(stream, layer, kind, d, tm, row_tile0, tn=None):
    if tn is None:
        return pl.BlockSpec(
            (None, None, 1, d),
            lambda i, j: (layer, stream.group_of_row((i + row_tile0) * tm), 0, kind))
    per = d // tn
    return pl.BlockSpec(
        (None, None, 1, tn),
        lambda i, j: (layer, stream.group_of_row((i + row_tile0) * tm), 0, kind * per + j))


def _linear_kernel(*refs, n_pro, n_epi, pro, epi, use_scratch):
    x_ref, w_ref = refs[0], refs[1]
    pro_refs = refs[2:2 + n_pro]
    epi_refs = refs[2 + n_pro:2 + n_pro + n_epi]
    o_ref = refs[2 + n_pro + n_epi]
    j = pl.program_id(1)
    if use_scratch:
        h_ref = refs[3 + n_pro + n_epi]

        @pl.when(j == 0)
        def _():
            h_ref[...] = pro(x_ref, *pro_refs)

        lhs = h_ref[...]
    else:
        lhs = x_ref[...]
    acc = _dot(lhs, w_ref[...])
    epi(acc, o_ref, j, *epi_refs)


def _epi_store(acc, o_ref, j):
    o_ref[...] = acc.astype(o_ref.dtype)


def linear(x, w, out_dtype, *, name, row_tile0=0, n_row_tiles=None, tm=ROW_TILE, tn=COL_TILE,
           pro=None, pro_args=(), epi=_epi_store, epi_args=(), out_rows=None,
           out_row_tile0=None, alias_epi_arg=None):
    k = x.shape[1]
    m = w.shape[1]
    tn = min(tn, m)
    if n_row_tiles is None:
        n_row_tiles = x.shape[0] // tm - row_tile0
    if out_row_tile0 is None:
        out_row_tile0 = row_tile0
    if out_rows is None:
        out_rows = x.shape[0]
    use_scratch = pro is not None
    kern = functools.partial(_linear_kernel, n_pro=len(pro_args), n_epi=len(epi_args),
                             pro=pro, epi=epi, use_scratch=use_scratch)
    in_specs = [
        pl.BlockSpec((tm, k), lambda i, j: (i + row_tile0, 0)),
        pl.BlockSpec((k, tn), lambda i, j: (0, j)),
    ] + [s for _, s in pro_args] + [s for _, s in epi_args]
    args = [x, w] + [a for a, _ in pro_args] + [a for a, _ in epi_args]
    aliases = {}
    if alias_epi_arg is not None:
        aliases = {2 + len(pro_args) + alias_epi_arg: 0}
    return pl.pallas_call(
        kern,
        grid=(n_row_tiles, m // tn),
        in_specs=in_specs,
        out_specs=pl.BlockSpec((tm, tn), lambda i, j: (i + out_row_tile0, j)),
        out_shape=jax.ShapeDtypeStruct((out_rows, m), out_dtype),
        scratch_shapes=[pltpu.VMEM((tm, k), BF16)] if use_scratch else [],
        input_output_aliases=aliases,
        compiler_params=_params("arbitrary", "arbitrary"),
        name=name,
    )(*args)


def _pro_norm_mod(x_ref, g_ref, shift_ref, scale_ref):
    y = _rms(x_ref[...]) * g_ref[...]
    return (y * (1.0 + scale_ref[...]) + shift_ref[...]).astype(BF16)


def norm_mod_args(stream, mod4, norm_g, layer, which, d, tm, row_tile0):
    g_spec = pl.BlockSpec((None, 1, d), lambda i, j: (layer, 0, 0))
    return (
        (norm_g, g_spec),
        (mod4, _mod_spec(stream, layer, 3 * which + 0, d, tm, row_tile0)),
        (mod4, _mod_spec(stream, layer, 3 * which + 1, d, tm, row_tile0)),
    )


def _epi_residual(acc, o_ref, j, res_ref, gate_ref):
    o_ref[...] = res_ref[...] + gate_ref[...] * acc


def _swap_pairs(x):
    n = x.shape[-1]
    lane = lax.broadcasted_iota(jnp.int32, x.shape, x.ndim - 1)
    nxt = pltpu.roll(x, n - 1, x.ndim - 1)
    prv = pltpu.roll(x, 1, x.ndim - 1)
    return jnp.where((lane & 1) == 0, nxt, prv)


def _rope(x, cos_ref, sin_ref):
    rep = x.shape[-1] // cos_ref.shape[-1]
    cos = cos_ref[...]
    sin = sin_ref[...]
    if rep > 1:
        cos = jnp.concatenate([cos] * rep, axis=-1)
        sin = jnp.concatenate([sin] * rep, axis=-1)
    return x * cos + _swap_pairs(x) * sin


def _epi_rope(acc, o_ref, j, cos_ref, sin_ref, *, n_rope_tiles):
    @pl.when(j < n_rope_tiles)
    def _():
        o_ref[...] = _rope(acc, cos_ref, sin_ref).astype(o_ref.dtype)

    @pl.when(j >= n_rope_tiles)
    def _():
        o_ref[...] = acc.astype(o_ref.dtype)


def _head_norm(acc, gain_ref):
    parts = []
    for k in range(acc.shape[-1] // LANES):
        parts.append(_rms(acc[:, k * LANES:(k + 1) * LANES]))
    return jnp.concatenate(parts, axis=-1) * gain_ref[...]


def _epi_headnorm(acc, o_ref, j, gain_ref, *, n_norm_tiles):
    @pl.when(j < n_norm_tiles)
    def _():
        o_ref[...] = _head_norm(acc, gain_ref).astype(o_ref.dtype)

    @pl.when(j >= n_norm_tiles)
    def _():
        o_ref[...] = acc.astype(o_ref.dtype)


def _epi_headnorm_rope(acc, o_ref, j, gain_ref, cos_ref, sin_ref, *, n_norm_tiles):
    @pl.when(j < n_norm_tiles)
    def _():
        o_ref[...] = _rope(_head_norm(acc, gain_ref), cos_ref, sin_ref).astype(o_ref.dtype)

    @pl.when(j >= n_norm_tiles)
    def _():
        o_ref[...] = acc.astype(o_ref.dtype)


def _epi_mla_down(acc, o_ref, j, qg_ref, kvg_ref, *rope_refs, q_lora, kv_lora):
    o_ref[:, :q_lora] = _rms(acc[:, :q_lora]) * qg_ref[...]
    o_ref[:, q_lora:q_lora + kv_lora] = _rms(acc[:, q_lora:q_lora + kv_lora]) * kvg_ref[...]
    kpe = acc[:, q_lora + kv_lora:]
    if rope_refs:
        kpe = _rope(kpe, *rope_refs)
    o_ref[:, q_lora + kv_lora:] = kpe


def _epi_rope_heads(acc, o_ref, j, cos_ref, sin_ref):
    parts = []
    for k in range(acc.shape[-1] // (2 * LANES)):
        parts.append(acc[:, 2 * k * LANES:(2 * k + 1) * LANES])
        parts.append(_rope(acc[:, (2 * k + 1) * LANES:(2 * k + 2) * LANES], cos_ref, sin_ref))
    o_ref[...] = jnp.concatenate(parts, axis=-1).astype(o_ref.dtype)


def rope_tables(rows, d_rot, width):
    t = jnp.arange(rows * GRID_W)
    row = (t // GRID_W).astype(F32)
    col = (t % GRID_W).astype(F32)
    d_axis = d_rot // 2
    inv = ROPE_THETA ** (-jnp.arange(0, d_axis, 2, dtype=F32) / d_axis)
    ang = jnp.concatenate([row[:, None] * inv, col[:, None] * inv], axis=-1)
    cos = jnp.repeat(jnp.cos(ang), 2, axis=-1)
    sin = jnp.repeat(jnp.sin(ang), 2, axis=-1)
    sign = jnp.where(jnp.arange(d_rot) % 2 == 0, -1.0, 1.0).astype(F32)
    sin = sin * sign
    if width > d_rot:
        cos = jnp.pad(cos, ((0, 0), (0, width - d_rot)), constant_values=1.0)
        sin = jnp.pad(sin, ((0, 0), (0, width - d_rot)))
    return cos, sin


def rope_args(cos, sin, stream, tm):
    per = stream.dec_seq // tm
    spec = pl.BlockSpec((tm, cos.shape[1]), lambda i, j: (i % per, 0))
    return ((cos, spec), (sin, spec))


def _softmax_pv(q, kv_refs, k_cols, scale):
    m = l = acc = None
    for k_ref, v_ref in kv_refs:
        t = k_ref.shape[0]
        kb = min(ATTN_KEY_BLOCK, t)
        for j in range(t // kb):
            k = k_ref[j * kb:(j + 1) * kb, k_cols].astype(BF16)
            v = v_ref[j * kb:(j + 1) * kb, :].astype(BF16)
            s = _dot_nt(q, k) * scale
            bm = s.max(axis=-1, keepdims=True)
            if m is None:
                m = bm
                p = jnp.exp(s - m)
                l = p.sum(axis=-1, keepdims=True)
                acc = _dot(p.astype(BF16), v)
            else:
                m_new = jnp.maximum(m, bm)
                alpha = jnp.exp(m - m_new)
                p = jnp.exp(s - m_new)
                l = alpha * l + p.sum(axis=-1, keepdims=True)
                acc = alpha * acc + _dot(p.astype(BF16), v)
                m = m_new
    return acc, l


def _attn_kernel(q_ref, *refs, n_seg, n_group, dq, dv, scale):
    kv_refs = [(refs[2 * s], refs[2 * s + 1]) for s in range(n_seg)]
    o_ref = refs[2 * n_seg]
    tq = q_ref.shape[0]
    q = jnp.concatenate([q_ref[:, g * dq:(g + 1) * dq] for g in range(n_group)], axis=0).astype(BF16)
    acc, l = _softmax_pv(q, kv_refs, slice(None), scale)
    o = acc / l
    for g in range(n_group):
        o_ref[:, g * dv:(g + 1) * dv] = o[g * tq:(g + 1) * tq].astype(o_ref.dtype)


def _diff_attn_kernel(lam_ref, sub_ref, q_ref, *refs, n_seg, d, scale, lam_init):
    kv_refs = [(refs[2 * s], refs[2 * s + 1]) for s in range(n_seg)]
    o_ref = refs[2 * n_seg]
    lp = lam_ref[...]
    lam = (jnp.exp(jnp.sum(lp[0:1] * lp[1:2], axis=-1, keepdims=True))
           - jnp.exp(jnp.sum(lp[2:3] * lp[3:4], axis=-1, keepdims=True)) + lam_init)
    os = []
    for c in range(2):
        q = q_ref[:, c * d:(c + 1) * d].astype(BF16)
        acc, l = _softmax_pv(q, kv_refs, slice(c * d, (c + 1) * d), scale)
        os.append(acc / l)
    o = os[0] - lam * os[1]
    o = _rms(o) * sub_ref[...] * (1.0 - lam_init)
    o_ref[...] = o.astype(o_ref.dtype)


def attention(q_arr, segs, *, n_batch, seq, n_kv_heads, n_group, dq, dv, q_col0, q_row0,
              out, out_row0, tq, kernel_fn, pre_args=(), name):
    n_q = seq // tq
    qw = n_group * dq
    ow = n_group * dv
    in_specs = [s for _, s in pre_args]
    args = [a for a, _ in pre_args]
    in_specs.append(pl.BlockSpec(
        (tq, qw), lambda b, h, i: ((q_row0 + b * seq) // tq + i, q_col0 // qw + h)))
    args.append(q_arr)
    for (k_arr, k_col0, dk, v_arr, v_col0, row0, t) in segs:
        in_specs.append(pl.BlockSpec(
            (t, dk), functools.partial(lambda b, h, i, row0, t, c0, w: (row0 // t + b, c0 // w + h),
                                       row0=row0, t=t, c0=k_col0, w=dk)))
        args.append(k_arr)
        in_specs.append(pl.BlockSpec(
            (t, dv), functools.partial(lambda b, h, i, row0, t, c0, w: (row0 // t + b, c0 // w + h),
                                       row0=row0, t=t, c0=v_col0, w=dv)))
        args.append(v_arr)
    in_specs.append(pl.BlockSpec(memory_space=pl.ANY))
    args.append(out)
    n_in = len(args)

    def kern(*refs):
        kernel_fn(*refs[:n_in - 1], refs[n_in])

    return pl.pallas_call(
        kern,
        grid=(n_batch, n_kv_heads, n_q),
        in_specs=in_specs,
        out_specs=pl.BlockSpec((tq, ow), lambda b, h, i: ((out_row0 + b * seq) // tq + i, h)),
        out_shape=jax.ShapeDtypeStruct(out.shape, out.dtype),
        input_output_aliases={n_in - 1: 0},
        compiler_params=_params("arbitrary", "arbitrary", "arbitrary"),
        name=name,
    )(*args)


def _ffn_kernel(x_ref, g_ref, shift_ref, scale_ref, gate_ref, wg_ref, wu_ref, wd_ref, o_ref,
                h_ref, acc_ref):
    f = pl.program_id(1)

    @pl.when(f == 0)
    def _():
        h_ref[...] = _pro_norm_mod(x_ref, g_ref, shift_ref, scale_ref)
        acc_ref[...] = jnp.zeros(acc_ref.shape, F32)

    h = h_ref[...]
    a = _silu(_dot(h, wg_ref[...])) * _dot(h, wu_ref[...])
    acc_ref[...] += _dot(a.astype(BF16), wd_ref[...])

    @pl.when(f == pl.num_programs(1) - 1)
    def _():
        o_ref[...] = x_ref[...] + gate_ref[...] * acc_ref[...]


def ffn_dense(x, stream, mod4, norm_g, layer, wg, wu, wd):
    n, d = x.shape
    f_dim = wg.shape[1]
    tm, tf = FFN_ROW_TILE, FFN_COL_TILE
    (g_a, g_s), (m_a, sh_s), (_, sc_s) = norm_mod_args(stream, mod4, norm_g, layer, 1, d, tm, 0)
    return pl.pallas_call(
        _ffn_kernel,
        grid=(n // tm, f_dim // tf),
        in_specs=[
            pl.BlockSpec((tm, d), lambda i, f: (i, 0)),
            g_s, sh_s, sc_s,
            _mod_spec(stream, layer, 5, d, tm, 0),
            pl.BlockSpec((d, tf), lambda i, f: (0, f)),
            pl.BlockSpec((d, tf), lambda i, f: (0, f)),
            pl.BlockSpec((tf, d), lambda i, f: (f, 0)),
        ],
        out_specs=pl.BlockSpec((tm, d), lambda i, f: (i, 0)),
        out_shape=jax.ShapeDtypeStruct((n, d), F32),
        scratch_shapes=[pltpu.VMEM((tm, d), BF16), pltpu.VMEM((tm, d), F32)],
        input_output_aliases={0: 0},
        compiler_params=_params("arbitrary", "arbitrary"),
        name="ffn_dense",
    )(x, g_a, m_a, m_a, m_a, wg, wu, wd)


def _router_kernel(x_ref, g_ref, shift_ref, scale_ref, wr_ref, h_ref, idx_ref, gate_ref):
    y = _rms(x_ref[...]) * g_ref[...]
    h = y * (1.0 + scale_ref[...]) + shift_ref[...]
    h_ref[...] = h.astype(BF16)
    logits = lax.dot_general(wr_ref[...], h, (((1,), (1,)), ((), ())),
                             precision=lax.Precision.HIGHEST,
                             preferred_element_type=F32)
    e_iota = lax.broadcasted_iota(jnp.int32, logits.shape, 0).astype(F32)
    m1 = logits.max(axis=0, keepdims=True)
    i1 = jnp.min(jnp.where(logits == m1, e_iota, float(N_EXPERTS)), axis=0, keepdims=True)
    rest = jnp.where(e_iota == i1, -jnp.inf, logits)
    m2 = rest.max(axis=0, keepdims=True)
    i2 = jnp.min(jnp.where(rest == m2, e_iota, float(N_EXPERTS)), axis=0, keepdims=True)
    e2 = jnp.exp(m2 - m1)
    den = 1.0 + e2
    idx_ref[...] = jnp.concatenate([i1, i2], axis=0).astype(jnp.int32)
    gate_ref[...] = jnp.concatenate([1.0 / den, e2 / den], axis=0)


def moe_router(x, stream, mod4, norm_g, layer, w_router_t):
    n, d = x.shape
    tm = ROW_TILE
    (g_a, g_s), (m_a, sh_s), (_, sc_s) = norm_mod_args(stream, mod4, norm_g, layer, 1, d, tm, 0)
    fix = lambda spec: pl.BlockSpec(spec.block_shape, lambda i, s=spec: s.index_map(i, 0))
    return pl.pallas_call(
        _router_kernel,
        grid=(n // tm,),
        in_specs=[
            pl.BlockSpec((tm, d), lambda i: (i, 0)),
            fix(g_s), fix(sh_s), fix(sc_s),
            pl.BlockSpec((N_EXPERTS, d), lambda i: (0, 0)),
        ],
        out_specs=[
            pl.BlockSpec((tm, d), lambda i: (i, 0)),
            pl.BlockSpec((2, tm), lambda i: (0, i)),
            pl.BlockSpec((2, tm), lambda i: (0, i)),
        ],
        out_shape=[
            jax.ShapeDtypeStruct((n, d), BF16),
            jax.ShapeDtypeStruct((2, n), jnp.int32),
            jax.ShapeDtypeStruct((2, n), F32),
        ],
        compiler_params=_params("arbitrary"),
        name="moe_router",
    )(x, g_a, m_a, m_a, w_router_t)


def _expert_changed(be_ref, nb_ref, i):
    last = nb_ref[0] - 1
    prev = jnp.maximum(i - 1, 0)
    return (i == 0) | (be_ref[jnp.minimum(i, last)] != be_ref[jnp.minimum(prev, last)])


def _moe_up_kernel(be_ref, nb_ref, x_ref, wg_ref, wu_ref, o_ref, w_ref):
    i = pl.program_id(1)
    tf = wg_ref.shape[1]

    @pl.when(_expert_changed(be_ref, nb_ref, i))
    def _():
        w_ref[:, :tf] = wg_ref[...].astype(BF16)
        w_ref[:, tf:] = wu_ref[...].astype(BF16)

    @pl.when(i < nb_ref[0])
    def _():
        gu = _dot(x_ref[...], w_ref[...])
        o_ref[...] = (_silu(gu[:, :tf]) * gu[:, tf:]).astype(o_ref.dtype)

    @pl.when(i >= nb_ref[0])
    def _():
        o_ref[...] = jnp.zeros(o_ref.shape, o_ref.dtype)


def _moe_down_kernel(be_ref, nb_ref, a_ref, wd_ref, o_ref, w_ref):
    i = pl.program_id(1)

    @pl.when(_expert_changed(be_ref, nb_ref, i))
    def _():
        w_ref[...] = wd_ref[...].astype(BF16)

    @pl.when(i < nb_ref[0])
    def _():
        o_ref[...] = _dot(a_ref[...], w_ref[...]).astype(o_ref.dtype)

    @pl.when(i >= nb_ref[0])
    def _():
        o_ref[...] = jnp.zeros(o_ref.shape, o_ref.dtype)


def moe_experts(xb, block_e, n_used, wg, wu, wd, lyr):
    cap, d = xb.shape
    f_dim = wg.shape[3]
    tm, tf, tn = MOE_ROW_TILE, MOE_COL_TILE, COL_TILE
    n_blocks = cap // tm

    def rows(j, i, be, nb):
        return (jnp.minimum(i, nb[0] - 1), 0)

    def wcol(j, i, be, nb):
        return (lyr, be[jnp.minimum(i, nb[0] - 1)], 0, j)

    up_spec = pltpu.PrefetchScalarGridSpec(
        num_scalar_prefetch=2,
        grid=(f_dim // tf, n_blocks),
        in_specs=[
            pl.BlockSpec((tm, d), rows),
            pl.BlockSpec((None, None, d, tf), wcol),
            pl.BlockSpec((None, None, d, tf), wcol),
        ],
        out_specs=pl.BlockSpec((tm, tf), lambda j, i, be, nb: (i, j)),
        scratch_shapes=[pltpu.VMEM((d, 2 * tf), BF16)],
    )
    a = pl.pallas_call(
        _moe_up_kernel,
        grid_spec=up_spec,
        out_shape=jax.ShapeDtypeStruct((cap, f_dim), BF16),
        compiler_params=_params("arbitrary", "arbitrary"),
        name="moe_up",
    )(block_e, n_used, xb, wg, wu)
    down_spec = pltpu.PrefetchScalarGridSpec(
        num_scalar_prefetch=2,
        grid=(d // tn, n_blocks),
        in_specs=[
            pl.BlockSpec((tm, f_dim), rows),
            pl.BlockSpec((None, None, f_dim, tn), wcol),
        ],
        out_specs=pl.BlockSpec((tm, tn), lambda j, i, be, nb: (i, j)),
        scratch_shapes=[pltpu.VMEM((f_dim, tn), BF16)],
    )
    return pl.pallas_call(
        _moe_down_kernel,
        grid_spec=down_spec,
        out_shape=jax.ShapeDtypeStruct((cap, d), F32),
        compiler_params=_params("arbitrary", "arbitrary"),
        name="moe_down",
    )(block_e, n_used, a, wd)


def _moe_combine_kernel(x_ref, gate_ref, y0_ref, y1_ref, w_ref, o_ref):
    w = w_ref[...]
    y = y0_ref[...] * w[:, 0:1] + y1_ref[...] * w[:, 1:2]
    o_ref[...] = x_ref[...] + gate_ref[...] * y


def moe_combine(x, stream, mod4, layer, y0, y1, gates_t):
    n, d = x.shape
    tm = FFN_ROW_TILE
    gspec = _mod_spec(stream, layer, 5, d, tm, 0)
    gspec = pl.BlockSpec(gspec.block_shape, lambda i, s=gspec: s.index_map(i, 0))
    row = pl.BlockSpec((tm, d), lambda i: (i, 0))
    return pl.pallas_call(
        _moe_combine_kernel,
        grid=(n // tm,),
        in_specs=[row, gspec, row, row, pl.BlockSpec((tm, 2), lambda i: (i, 0))],
        out_specs=row,
        out_shape=jax.ShapeDtypeStruct((n, d), F32),
        input_output_aliases={0: 0},
        compiler_params=_params("arbitrary"),
        name="moe_combine",
    )(x, mod4, y0, y1, gates_t)


def moe_layer(x, stream, mod4, norm_g, layer, w_router, wg, wu, wd, lyr):
    n, d = x.shape
    tm = MOE_ROW_TILE
    h, idx, gates = moe_router(x, stream, mod4, norm_g, layer, w_router.T)
    slot_e = idx.reshape(-1)
    n_slots = slot_e.shape[0]
    onehot = (slot_e[:, None] == jnp.arange(N_EXPERTS)[None, :]).astype(jnp.int32)
    rank = jnp.take_along_axis(jnp.cumsum(onehot, axis=0) - onehot, slot_e[:, None], axis=1)[:, 0]
    counts = onehot.sum(axis=0)
    padded = (counts + tm - 1) // tm * tm
    pad_end = jnp.cumsum(padded)
    pad_start = pad_end - padded
    dest = pad_start[slot_e] + rank
    cap = (n_slots // tm + N_EXPERTS) * tm
    n_blocks = cap // tm
    slot_tok = jnp.arange(n_slots, dtype=jnp.int32) % n
    buf_tok = jnp.zeros((cap,), jnp.int32).at[dest].set(slot_tok)
    block_start = jnp.arange(n_blocks) * tm
    block_e = jnp.minimum(jnp.sum(pad_end[None, :] <= block_start[:, None], axis=-1),
                          N_EXPERTS - 1).astype(jnp.int32)
    n_used = (pad_end[-1] // tm).astype(jnp.int32).reshape(1)
    xb = jnp.take(h, buf_tok, axis=0)
    yb = moe_experts(xb, block_e, n_used, wg, wu, wd, lyr)
    y0 = jnp.take(yb, dest[:n], axis=0)
    y1 = jnp.take(yb, dest[n:], axis=0)
    return moe_combine(x, stream, mod4, layer, y0, y1, gates.T)


def _gla_gate_kernel(gl_ref, w_ref, b_ref, gf_ref, gb_ref):
    gl = gl_ref[...]
    for z, o_ref in enumerate((gf_ref, gb_ref)):
        a = gl[:, z * GLA_GATE_RANK:(z + 1) * GLA_GATE_RANK]
        t = jnp.dot(a, w_ref[z], precision=lax.Precision.HIGHEST,
                    preferred_element_type=F32) + b_ref[z]
        o_ref[...] = (jnp.minimum(t, 0.0) - jnp.log1p(jnp.exp(-jnp.abs(t)))) / GLA_TAU


def gla_gates(gl, w_gate2, b_gate):
    n = gl.shape[0]
    kw = w_gate2.shape[2]
    tm = ROW_TILE
    out = jax.ShapeDtypeStruct((n, kw), F32)
    return pl.pallas_call(
        _gla_gate_kernel,
        grid=(n // tm,),
        in_specs=[
            pl.BlockSpec((tm, gl.shape[1]), lambda i: (i, 0)),
            pl.BlockSpec(w_gate2.shape, lambda i: (0, 0, 0)),
            pl.BlockSpec((2, 1, kw), lambda i: (0, 0, 0)),
        ],
        out_specs=[pl.BlockSpec((tm, kw), lambda i: (i, 0))] * 2,
        out_shape=[out, out],
        compiler_params=_params("arbitrary"),
        name="gla_gates",
    )(gl, w_gate2, b_gate.reshape(2, 1, kw))


def _cumsum_rows(g, tri):
    hi = g.astype(BF16)
    r1 = g - hi.astype(F32)
    mid = r1.astype(BF16)
    lo = (r1 - mid.astype(F32)).astype(BF16)
    return _dot(tri, hi) + _dot(tri, mid) + _dot(tri, lo)


def _row_to_col(x):
    n = x.shape[1]
    eye = lax.broadcasted_iota(jnp.int32, (n, n), 0) == lax.broadcasted_iota(jnp.int32, (n, n), 1)
    return jnp.sum(jnp.where(eye, jnp.broadcast_to(x, (n, n)), 0.0), axis=1, keepdims=True)


def _gla_intra(q, k, b, *, reverse):
    c = q.shape[0]
    n_sub = c // GLA_SUB
    lane = lax.broadcasted_iota(jnp.int32, (GLA_SUB, c), 1)
    row = lax.broadcasted_iota(jnp.int32, (GLA_SUB, 1), 0)
    blocks = []
    for bi in range(n_sub):
        lo, hi = bi * GLA_SUB, (bi + 1) * GLA_SUB
        q_i, b_i = q[lo:hi], b[lo:hi]
        has_before = bi < n_sub - 1 if reverse else bi > 0
        if has_before:
            anchor = b[hi:hi + 1] if reverse else b[lo - 1:lo]
            q_a = (q_i * jnp.exp(b_i - anchor)).astype(BF16)
            k_a = (k * jnp.exp(jnp.minimum(anchor - b, 0.0))).astype(BF16)
            before = lane >= hi if reverse else lane < lo
            blk = jnp.where(before, _dot_nt(q_a, k_a), 0.0)
        else:
            blk = jnp.zeros((GLA_SUB, c), F32)
        for jj in range(GLA_SUB):
            j = lo + jj
            w = q_i * k[j:j + 1] * jnp.exp(jnp.minimum(b_i - b[j:j + 1], 0.0))
            col = jnp.sum(w, axis=-1, keepdims=True)
            col = jnp.where(row <= jj if reverse else row >= jj, col, 0.0)
            blk = jnp.where(lane == j, col, blk)
        blocks.append(blk)
    return jnp.concatenate(blocks, axis=0)


def _gla_chunk(q, k, v, g, s, *, reverse):
    c = q.shape[0]
    r_i = lax.broadcasted_iota(jnp.int32, (c, c), 0)
    c_i = lax.broadcasted_iota(jnp.int32, (c, c), 1)
    earlier = (c_i >= r_i) if reverse else (c_i <= r_i)
    tri = jnp.where(earlier, 1.0, 0.0).astype(BF16)
    b = _cumsum_rows(g, tri)
    b_tot = b[0:1] if reverse else b[c - 1:c]
    o = _dot((q * jnp.exp(b)).astype(BF16), s.astype(BF16))
    att = _gla_intra(q, k, b, reverse=reverse)
    o = o + _dot(att.astype(BF16), v)
    k_dec = (k * jnp.exp(b_tot - b)).astype(BF16)
    s_new = _row_to_col(jnp.exp(b_tot)) * s + _dot_tn(k_dec, v)
    return o, s_new


def _gla_kernel(q_ref, k_ref, v_ref, r_ref, gf_ref, gb_ref, s0_ref, ng_ref, o_ref, s_ref,
                accf_ref, accb_ref, *, seq, scale):
    n_chunks = seq // GLA_CHUNK
    s_ref[...] = s0_ref[...]

    def body(t, carry):
        for reverse, g_ref, acc_ref in ((False, gf_ref, accf_ref), (True, gb_ref, accb_ref)):
            ci = (n_chunks - 1 - t) if reverse else t
            rows = pl.ds(pl.multiple_of(ci * GLA_CHUNK, GLA_CHUNK), GLA_CHUNK)
            q = q_ref[rows, :].astype(F32) * scale
            k = k_ref[rows, :].astype(F32)
            v = v_ref[rows, :].astype(BF16)
            d = 1 if reverse else 0
            o, s_new = _gla_chunk(q, k, v, g_ref[rows, :], s_ref[d], reverse=reverse)
            acc_ref[rows, :] = o
            s_ref[d] = s_new
        return carry

    lax.fori_loop(0, n_chunks, body, 0)
    o = _rms(accf_ref[...] + accb_ref[...]) * ng_ref[...]
    o_ref[...] = (o * _silu(r_ref[...].astype(F32))).astype(o_ref.dtype)


def gla_core(proj, gf, gb, state, norm_g, *, n_batch, seq, n_heads, dk, dv, row0, g_row0, out,
             out_row0, name):
    kw = n_heads * dk
    vw = n_heads * dv
    rb = row0 // seq
    gr = g_row0 // seq
    ob = out_row0 // seq
    in_specs = [
        pl.BlockSpec((seq, dk), lambda b, h: (rb + b, h)),
        pl.BlockSpec((seq, dk), lambda b, h: (rb + b, kw // dk + h)),
        pl.BlockSpec((seq, dv), lambda b, h: (rb + b, 2 * kw // dv + h)),
        pl.BlockSpec((seq, dv), lambda b, h: (rb + b, (2 * kw + vw) // dv + h)),
        pl.BlockSpec((seq, dk), lambda b, h: (gr + b, h)),
        pl.BlockSpec((seq, dk), lambda b, h: (gr + b, h)),
        pl.BlockSpec((None, 2, None, dk, dv), lambda b, h: (b, 0, h, 0, 0)),
        pl.BlockSpec((1, dv), lambda b, h: (0, 0)),
        pl.BlockSpec(memory_space=pl.ANY),
    ]

    def kern(q, k, v, r, gf_, gb_, s0, ng, _alias, o, s_out, acc_f, acc_b):
        _gla_kernel(q, k, v, r, gf_, gb_, s0, ng, o, s_out, acc_f, acc_b, seq=seq, scale=dk ** -0.5)

    return pl.pallas_call(
        kern,
        grid=(n_batch, n_heads),
        in_specs=in_specs,
        out_specs=[
            pl.BlockSpec((seq, dv), lambda b, h: (ob + b, h)),
            pl.BlockSpec((None, 2, None, dk, dv), lambda b, h: (b, 0, h, 0, 0)),
        ],
        out_shape=[
            jax.ShapeDtypeStruct(out.shape, out.dtype),
            jax.ShapeDtypeStruct(state.shape, F32),
        ],
        scratch_shapes=[pltpu.VMEM((seq, dv), F32), pltpu.VMEM((seq, dv), F32)],
        input_output_aliases={8: 0},
        compiler_params=_params("arbitrary", "arbitrary"),
        name=name,
    )(proj, proj, proj, proj, gf, gb, state, norm_g.reshape(1, dv), out)


def _final_norm_kernel(x_ref, g_ref, o_ref):
    o_ref[...] = _rms(x_ref[...]) * g_ref[...]


def final_norm(x, g, row0, rows):
    d = x.shape[1]
    tm = ROW_TILE
    return pl.pallas_call(
        _final_norm_kernel,
        grid=(rows // tm,),
        in_specs=[pl.BlockSpec((tm, d), lambda i: (row0 // tm + i, 0)),
                  pl.BlockSpec((1, d), lambda i: (0, 0))],
        out_specs=pl.BlockSpec((tm, d), lambda i: (i, 0)),
        out_shape=jax.ShapeDtypeStruct((rows, d), F32),
        compiler_params=_params("arbitrary"),
        name="final_norm",
    )(x, g.reshape(1, d))


def kernel(x_prompt, x_sample, cache_diff_k, cache_diff_v, cache_mla_ckv, cache_mla_kpe,
           cache_gqa_k, cache_gqa_v, state_gla, c, c_ctx, norm_mix_g, norm_ffn_g, w_mod, b_mod,
           diff_w_qkv, diff_lambda, diff_subln_g, diff_w_o,
           mla_w_down, mla_q_norm_g, mla_kv_norm_g, mla_w_uq, mla_w_ukv, mla_w_o,
           gqa_w_qkv, gqa_q_norm_g, gqa_k_norm_g, gqa_w_o,
           gla_w_in, gla_w_gate2, gla_b_gate, gla_norm_g, gla_w_o,
           ffn_w_gate, ffn_w_up, ffn_w_down,
           moe_w_router, moe_w_gate, moe_w_up, moe_w_down, final_norm_g):
    n_b, seq, d = x_prompt.shape
    dec_b, dec_seq, _ = x_sample.shape
    depth = norm_mix_g.shape[0]
    past = cache_diff_k.shape[2]
    stream = Stream(n_b * seq, dec_b, dec_seq)
    n_p, n_s, n = stream.n_prompt, stream.n_sample, stream.n
    tm = ROW_TILE
    p_tiles, s_tiles = n_p // tm, n_s // tm
    rows = dec_seq // GRID_W
    assert n_p % dec_seq == 0 and n_p % tm == 0 and dec_seq % tm == 0 and seq % GLA_CHUNK == 0

    x = jnp.concatenate([x_prompt.reshape(n_p, d), x_sample.reshape(n_s, d)], axis=0)

    n_groups = 16
    cvec = jnp.zeros((n_groups, d), F32).at[0].set(c_ctx).at[1:1 + dec_b].set(c)
    mod = modulation_all(cvec, w_mod, b_mod)
    mod4 = mod.reshape(depth, n_groups, 1, 6 * d)
    norm_mix3 = norm_mix_g.reshape(depth, 1, d)
    norm_ffn3 = norm_ffn_g.reshape(depth, 1, d)

    cos_h, sin_h = rope_tables(rows, 128, 128)
    cos_m, sin_m = rope_tables(rows, 64, 128)

    def mix_in(layer, w, out_dtype_p, out_dtype_s, epi_p, epi_args_p, epi_s, epi_args_s, name,
               tn=COL_TILE):
        outs = []
        for (t0, nt, odt, epi, eargs, tag) in (
                (0, p_tiles, out_dtype_p, epi_p, epi_args_p, "p"),
                (p_tiles, s_tiles, out_dtype_s, epi_s, epi_args_s, "s")):
            outs.append(linear(
                x, w, odt, name=f"{name}_{tag}", row_tile0=t0, n_row_tiles=nt, tn=tn,
                pro=_pro_norm_mod,
                pro_args=norm_mod_args(stream, mod4, norm_mix3, layer, 0, d, tm, t0),
                epi=epi, epi_args=eargs, out_rows=nt * tm, out_row_tile0=0))
        return outs

    def mix_out(layer, o, w_o, name):
        return linear(
            o, w_o, F32, name=name, epi=_epi_residual,
            epi_args=((x, pl.BlockSpec((tm, COL_TILE), lambda i, j: (i, j))),
                      (mod4, _mod_spec(stream, layer, 2, d, tm, 0, tn=COL_TILE))),
            alias_epi_arg=0)

    diff_k_l, diff_v_l, mla_ckv_l, mla_kpe_l, gqa_k_l, gqa_v_l, gla_s_l = [], [], [], [], [], [], []
    for i in range(depth):
        j = i // 4
        kind = i % 4
        o_buf = jnp.zeros((n, d), BF16)
        if kind == 0:
            hd = diff_lambda.shape[-1]
            n_h = d // (2 * hd)
            qk_w = n_h * 2 * hd
            lam_init = 0.8 - 0.6 * math.exp(-0.3 * i)
            w = diff_w_qkv[j].astype(BF16)
            qkv_p, qkv_s = mix_in(
                i, w, F32, BF16, _epi_store, (),
                functools.partial(_epi_rope, n_rope_tiles=2 * qk_w // COL_TILE),
                rope_args(cos_h, sin_h, stream, tm), "diff_qkv")
            diff_k_l.append(qkv_p[:, qk_w:2 * qk_w].reshape(n_b, seq, n_h, 2, hd))
            diff_v_l.append(qkv_p[:, 2 * qk_w:].reshape(n_b, seq, n_h, 2 * hd))
            ck = cache_diff_k[:, j].reshape(dec_b * past, qk_w)
            cv = cache_diff_v[:, j].reshape(dec_b * past, qk_w)
            pre = ((diff_lambda[j], pl.BlockSpec((4, hd), lambda b, h, t: (0, 0))),
                   (diff_subln_g[j].reshape(1, 2 * hd), pl.BlockSpec((1, 2 * hd), lambda b, h, t: (0, 0))))
            scale = hd ** -0.5
            o_buf = attention(
                qkv_p, [(qkv_p, qk_w, 2 * hd, qkv_p, 2 * qk_w, 0, seq)],
                n_batch=n_b, seq=seq, n_kv_heads=n_h, n_group=1, dq=2 * hd, dv=2 * hd, q_col0=0,
                q_row0=0, out=o_buf, out_row0=0, tq=seq, pre_args=pre, name="diff_attn_p",
                kernel_fn=functools.partial(_diff_attn_kernel, n_seg=1, d=hd, scale=scale,
                                            lam_init=lam_init))
            o_buf = attention(
                qkv_s, [(qkv_s, qk_w, 2 * hd, qkv_s, 2 * qk_w, 0, dec_seq),
                        (ck, 0, 2 * hd, cv, 0, 0, past)],
                n_batch=dec_b, seq=dec_seq, n_kv_heads=n_h, n_group=1, dq=2 * hd, dv=2 * hd,
                q_col0=0, q_row0=0, out=o_buf, out_row0=n_p, tq=ATTN_Q_ROWS, pre_args=pre,
                name="diff_attn_s",
                kernel_fn=functools.partial(_diff_attn_kernel, n_seg=2, d=hd, scale=scale,
                                            lam_init=lam_init))
            x = mix_out(i, o_buf, diff_w_o[j].astype(BF16), "diff_out")
        elif kind == 1:
            q_lora = mla_q_norm_g.shape[1]
            kv_lora = mla_kv_norm_g.shape[1]
            d_rope = cache_mla_kpe.shape[-1]
            n_h = d // 128
            d_nope = mla_w_uq.shape[2] // n_h - d_rope
            d_v = mla_w_ukv.shape[2] // n_h - d_nope
            hw = 2 * LANES
            w_down = jnp.pad(mla_w_down[j], ((0, 0), (0, LANES - d_rope))).astype(BF16)
            dw = w_down.shape[1]
            g_args = ((mla_q_norm_g[j].reshape(1, q_lora), pl.BlockSpec((1, q_lora), lambda i_, j_: (0, 0))),
                      (mla_kv_norm_g[j].reshape(1, kv_lora), pl.BlockSpec((1, kv_lora), lambda i_, j_: (0, 0))))
            epi_down = functools.partial(_epi_mla_down, q_lora=q_lora, kv_lora=kv_lora)
            down_p, down_s = mix_in(i, w_down, F32, F32, epi_down, g_args, epi_down,
                                    g_args + rope_args(cos_m, sin_m, stream, tm), "mla_down", tn=dw)
            mla_ckv_l.append(down_p[:, q_lora:q_lora + kv_lora].reshape(n_b, seq, kv_lora))
            mla_kpe_l.append(down_p[:, q_lora + kv_lora:q_lora + kv_lora + d_rope].reshape(n_b, seq, d_rope))
            wq = mla_w_uq[j].reshape(q_lora, n_h, d_nope + d_rope)
            wq = jnp.pad(wq, ((0, 0), (0, 0), (0, hw - d_nope - d_rope))).reshape(q_lora, n_h * hw).astype(BF16)
            wkv = mla_w_ukv[j].reshape(kv_lora, n_h, d_nope + d_v)
            wk = jnp.pad(wkv[:, :, :d_nope], ((0, 0), (0, 0), (0, hw - d_nope)))
            eye = jnp.pad(jnp.eye(d_rope, dtype=F32), ((0, 0), (d_nope, hw - d_nope - d_rope)))
            wk = jnp.concatenate([wk, jnp.broadcast_to(eye[:, None, :], (d_rope, n_h, hw))], axis=0)
            wk = wk.reshape(kv_lora + d_rope, n_h * hw).astype(BF16)
            wv = wkv[:, :, d_nope:].reshape(kv_lora, n_h * d_v).astype(BF16)
            q_p = linear(down_p[:, :q_lora].astype(BF16), wq, BF16, name="mla_uq_p")
            q_s = linear(down_s[:, :q_lora].astype(BF16), wq, BF16, name="mla_uq_s",
                         epi=_epi_rope_heads, epi_args=rope_args(cos_m, sin_m, stream, tm))
            u_p = down_p[:, q_lora:q_lora + kv_lora + d_rope]
            u_new = down_s[:, q_lora:q_lora + kv_lora + d_rope].reshape(dec_b, dec_seq, -1)
            u_old = jnp.concatenate([cache_mla_ckv[:, j], cache_mla_kpe[:, j]], axis=-1)
            t_all = dec_seq + past
            u_s = jnp.concatenate([u_new, u_old], axis=1).reshape(dec_b * t_all, -1)
            ukv_tm = 512
            k_p = linear(u_p.astype(BF16), wk, BF16, name="mla_uk_p", tm=ukv_tm)
            v_p = linear(u_p[:, :kv_lora].astype(BF16), wv, BF16, name="mla_uv_p", tm=ukv_tm)
            k_s = linear(u_s.astype(BF16), wk, BF16, name="mla_uk_s", tm=ukv_tm)
            v_s = linear(u_s[:, :kv_lora].astype(BF16), wv, BF16, name="mla_uv_s", tm=ukv_tm)
            scale = (d_nope + d_rope) ** -0.5
            kf = functools.partial(_attn_kernel, n_seg=1, n_group=1, dq=hw, dv=d_v, scale=scale)
            o_buf = attention(
                q_p, [(k_p, 0, hw, v_p, 0, 0, seq)],
                n_batch=n_b, seq=seq, n_kv_heads=n_h, n_group=1, dq=hw, dv=d_v, q_col0=0, q_row0=0,
                out=o_buf, out_row0=0, tq=seq, kernel_fn=kf, name="mla_attn_p")
            o_buf = attention(
                q_s, [(k_s, 0, hw, v_s, 0, 0, t_all)],
                n_batch=dec_b, seq=dec_seq, n_kv_heads=n_h, n_group=1, dq=hw, dv=d_v, q_col0=0,
                q_row0=0, out=o_buf, out_row0=n_p, tq=ATTN_Q_ROWS, kernel_fn=kf, name="mla_attn_s")
            x = mix_out(i, o_buf, mla_w_o[j].astype(BF16), "mla_out")
        elif kind == 2:
            hd = gqa_q_norm_g.shape[1]
            n_kv = cache_gqa_k.shape[3]
            n_h = d // hd
            grp = n_h // n_kv
            q_w, kv_w = n_h * hd, n_kv * hd
            w = gqa_w_qkv[j].astype(BF16)
            gain = jnp.concatenate([jnp.tile(gqa_q_norm_g[j], n_h), jnp.tile(gqa_k_norm_g[j], n_kv),
                                    jnp.ones((kv_w,), F32)]).reshape(1, -1)
            gain_arg = ((gain, pl.BlockSpec((1, COL_TILE), lambda i_, j_: (0, j_))),)
            n_norm = (q_w + kv_w) // COL_TILE
            qkv_p, qkv_s = mix_in(
                i, w, F32, BF16,
                functools.partial(_epi_headnorm, n_norm_tiles=n_norm), gain_arg,
                functools.partial(_epi_headnorm_rope, n_norm_tiles=n_norm),
                gain_arg + rope_args(cos_h, sin_h, stream, tm), "gqa_qkv")
            gqa_k_l.append(qkv_p[:, q_w:q_w + kv_w].reshape(n_b, seq, n_kv, hd))
            gqa_v_l.append(qkv_p[:, q_w + kv_w:].reshape(n_b, seq, n_kv, hd))
            ck = cache_gqa_k[:, j].reshape(dec_b * past, kv_w)
            cv = cache_gqa_v[:, j].reshape(dec_b * past, kv_w)
            scale = hd ** -0.5
            o_buf = attention(
                qkv_p, [(qkv_p, q_w, hd, qkv_p, q_w + kv_w, 0, seq)],
                n_batch=n_b, seq=seq, n_kv_heads=n_kv, n_group=grp, dq=hd, dv=hd, q_col0=0, q_row0=0,
                out=o_buf, out_row0=0, tq=min(seq, ATTN_Q_ROWS // grp), name="gqa_attn_p",
                kernel_fn=functools.partial(_attn_kernel, n_seg=1, n_group=grp, dq=hd, dv=hd, scale=scale))
            o_buf = attention(
                qkv_s, [(qkv_s, q_w, hd, qkv_s, q_w + kv_w, 0, dec_seq), (ck, 0, hd, cv, 0, 0, past)],
                n_batch=dec_b, seq=dec_seq, n_kv_heads=n_kv, n_group=grp, dq=hd, dv=hd, q_col0=0,
                q_row0=0, out=o_buf, out_row0=n_p, tq=ATTN_Q_ROWS // grp, name="gqa_attn_s",
                kernel_fn=functools.partial(_attn_kernel, n_seg=2, n_group=grp, dq=hd, dv=hd, scale=scale))
            x = mix_out(i, o_buf, gqa_w_o[j].astype(BF16), "gqa_out")
        else:
            n_h = state_gla.shape[3]
            dk, dv = state_gla.shape[4], state_gla.shape[5]
            kw, vw = n_h * dk, n_h * dv
            main_w = 2 * kw + 2 * vw
            w_in = gla_w_in[j]
            w_main = w_in[:, :main_w].astype(BF16)
            w_gl = jnp.pad(w_in[:, main_w:], ((0, 0), (0, LANES - 2 * GLA_GATE_RANK))).astype(BF16)
            pro_all = norm_mod_args(stream, mod4, norm_mix3, i, 0, d, tm, 0)
            proj = linear(x, w_main, BF16, name="gla_in", pro=_pro_norm_mod, pro_args=pro_all)
            gl = linear(x, w_gl, F32, name="gla_in_gate", pro=_pro_norm_mod, pro_args=pro_all)
            gf, gb = gla_gates(gl, gla_w_gate2[j], gla_b_gate[j])
            zero = jnp.zeros((n_b, 2, n_h, dk, dv), F32)
            o_buf, s_new = gla_core(proj, gf, gb, zero, gla_norm_g[j], n_batch=n_b, seq=seq,
                                    n_heads=n_h, dk=dk, dv=dv, row0=0, g_row0=0, out=o_buf,
                                    out_row0=0, name="gla_p")
            o_buf, _ = gla_core(proj, gf, gb, state_gla[:, j], gla_norm_g[j], n_batch=dec_b,
                                seq=dec_seq, n_heads=n_h, dk=dk, dv=dv, row0=n_p, g_row0=n_p,
                                out=o_buf, out_row0=n_p, name="gla_s")
            gla_s_l.append(s_new)
            x = mix_out(i, o_buf, gla_w_o[j].astype(BF16), "gla_out")

        f = i // 2
        if i % 2 == 0:
            x = ffn_dense(x, stream, mod4, norm_ffn3, i, ffn_w_gate[f].astype(BF16),
                          ffn_w_up[f].astype(BF16), ffn_w_down[f].astype(BF16))
        else:
            x = moe_layer(x, stream, mod4, norm_ffn3, i, moe_w_router[f], moe_w_gate, moe_w_up,
                          moe_w_down, f)

    y_prompt = final_norm(x, final_norm_g, 0, n_p).reshape(n_b, seq, d)
    y_sample = final_norm(x, final_norm_g, n_p, n_s).reshape(dec_b, dec_seq, d)
    return (y_prompt, y_sample,
            jnp.stack(diff_k_l, axis=1), jnp.stack(diff_v_l, axis=1),
            jnp.stack(mla_ckv_l, axis=1), jnp.stack(mla_kpe_l, axis=1),
            jnp.stack(gqa_k_l, axis=1), jnp.stack(gqa_v_l, axis=1),
            jnp.stack(gla_s_l, axis=1))
```
